```python
import math
import jax, jax.numpy as jnp
from jax import lax
import numpy as np

D_MODEL = 2048
BATCH = 4
SEQ = 2048
DEPTH = 2
DEC_BATCH = 128
DEC_SEQ = 1
PAST_LEN = 16384
PAGE_SIZE = 128

D_MIX = D_MODEL
W_A = D_MIX // 2
W_B = D_MIX - W_A
HD_A = 128
H_A = W_A // HD_A
G_B = 8
CHUNK = 128
CONV_W = 31
D_IN = 3 * W_A + 3 * W_B
EPS = 1e-6

kernel_name = "hybrid_gmlp_conformer_adaln_step"


def _rmsnorm(x, g):
    xf = x.astype(jnp.float32)
    y = xf * lax.rsqrt(jnp.mean(xf * xf, axis=-1, keepdims=True) + EPS)
    return (y * g).astype(x.dtype)


def _layernorm_groups(x, g, b, n_groups):
    shp = x.shape
    xf = x.astype(jnp.float32).reshape(*shp[:-1], n_groups, shp[-1] // n_groups)
    mu = jnp.mean(xf, axis=-1, keepdims=True)
    xc = xf - mu
    var = jnp.mean(xc * xc, axis=-1, keepdims=True)
    y = (xc * lax.rsqrt(var + EPS)).reshape(shp)
    return (y * g + b).astype(x.dtype)


def _chunk_spatial(vh, w_s, b_s):
    n, L, h, e = vh.shape
    n_chunks = -(-L // CHUNK)
    lp = n_chunks * CHUNK
    vp = jnp.pad(vh, ((0, 0), (0, lp - L), (0, 0), (0, 0))).reshape(n, n_chunks, CHUNK, h, e)
    mask = jnp.tril(jnp.ones((CHUNK, CHUNK), dtype=bool))
    wm = jnp.where(mask[None], w_s, jnp.zeros_like(w_s))
    out = jnp.einsum('hts,ncshe->ncthe', wm, vp) + b_s.T[None, None, :, :, None]
    return out.reshape(n, lp, h, e)[:, :L]


def _causal_dwconv(xpad, w, bias):
    c = xpad.shape[-1]
    out = lax.conv_general_dilated(xpad, w[:, None, :], window_strides=(1,), padding='VALID',
                                   dimension_numbers=('NWC', 'WIO', 'NWC'), feature_group_count=c)
    return out + bias


def _layer(x, c, buf, g_norm, w_ada, b_ada, w_in, w_s, b_s, g_v, b_v,
           w_conv, b_conv, g_cn, b_cn, w_out):
    n, L, d = x.shape
    mod = jax.nn.silu(c) @ w_ada + b_ada
    shift, scale, gate = jnp.split(mod[:, None, :], 3, axis=-1)
    h = _rmsnorm(x, g_norm) * (1 + scale) + shift
    proj = h @ w_in
    u, v, z_a, a, b, z_b = jnp.split(
        proj, np.cumsum([W_A, W_A, W_A, W_B, W_B]).tolist(), axis=-1)
    u = jax.nn.gelu(u, approximate=False)
    vn = _layernorm_groups(jax.nn.gelu(v, approximate=False), g_v, b_v, H_A)
    vh = vn.reshape(n, L, H_A, HD_A)
    y_a = u * _chunk_spatial(vh, w_s, b_s).reshape(n, L, W_A)
    glu = a * jax.nn.sigmoid(b)
    xpad = jnp.concatenate([buf.astype(glu.dtype), glu], axis=1)
    conv = _causal_dwconv(xpad, w_conv, b_conv)
    y_b = jax.nn.silu(_layernorm_groups(conv, g_cn, b_cn, G_B))
    mixed = jnp.concatenate([y_a * jax.nn.silu(z_a), y_b * jax.nn.silu(z_b)], axis=-1) @ w_out
    x = x + gate * mixed
    start = ((L - 1) // CHUNK) * CHUNK
    return x, vh[:, start:], xpad[:, -(CONV_W - 1):]


def setup_inputs(seed: int = 0) -> dict:
    key = jax.random.key(seed)
    ks = jax.random.split(key, 24)
    f32 = jnp.float32
    nrm = lambda k, s, sc: jax.random.normal(k, s, f32) * sc
    return {
        "x_prompt": nrm(ks[0], (BATCH, SEQ, D_MODEL), 1.0),
        "x_sample": nrm(ks[1], (DEC_BATCH, DEC_SEQ, D_MODEL), 1.0),
        "c_prompt": nrm(ks[2], (BATCH, D_MODEL), 1.0),
        "c_sample": nrm(ks[3], (DEC_BATCH, D_MODEL), 1.0),
        "state_conv": nrm(ks[4], (DEPTH, DEC_BATCH, CONV_W - 1, W_B), 0.5),
        "g_norm": 1.0 + nrm(ks[5], (DEPTH, D_MODEL), 0.02),
        "w_ada": nrm(ks[6], (DEPTH, D_MODEL, 3 * D_MODEL), 0.5 * D_MODEL ** -0.5),
        "b_ada": nrm(ks[7], (DEPTH, 3 * D_MODEL), 0.02),
        "w_in": nrm(ks[8], (DEPTH, D_MODEL, D_IN), D_MODEL ** -0.5),
        "w_s": nrm(ks[9], (DEPTH, H_A, CHUNK, CHUNK), CHUNK ** -0.5),
        "b_s": 1.0 + nrm(ks[10], (DEPTH, H_A, CHUNK), 0.02),
        "g_v": 1.0 + nrm(ks[11], (DEPTH, W_A), 0.02),
        "b_v": nrm(ks[12], (DEPTH, W_A), 0.02),
        "w_conv": nrm(ks[13], (DEPTH, CONV_W, W_B), CONV_W ** -0.5),
        "b_conv": nrm(ks[14], (DEPTH, W_B), 0.02),
        "g_cn": 1.0 + nrm(ks[15], (DEPTH, W_B), 0.02),
        "b_cn": nrm(ks[16], (DEPTH, W_B), 0.02),
        "w_out": nrm(ks[17], (DEPTH, D_MIX, D_MODEL), D_MIX ** -0.5),
        "g_final": 1.0 + nrm(ks[18], (D_MODEL,), 0.02),
    }


def reference(x_prompt, x_sample, c_prompt, c_sample, state_conv, g_norm, w_ada, b_ada,
              w_in, w_s, b_s, g_v, b_v, w_conv, b_conv, g_cn, b_cn, w_out, g_final):
    xp, xs = x_prompt, x_sample
    buf_p = jnp.zeros((xp.shape[0], CONV_W - 1, W_B), dtype=xp.dtype)
    vp_l, cp_l, vs_l, cs_l = [], [], [], []
    for l in range(DEPTH):
        params = (g_norm[l], w_ada[l], b_ada[l], w_in[l], w_s[l], b_s[l], g_v[l], b_v[l],
                  w_conv[l], b_conv[l], g_cn[l], b_cn[l], w_out[l])
        xp, vp, cp = _layer(xp, c_prompt, buf_p, *params)
        xs, vs, cs = _layer(xs, c_sample, state_conv[l], *params)
        vp_l.append(vp); cp_l.append(cp); vs_l.append(vs); cs_l.append(cs)
    y_prompt = _rmsnorm(xp, g_final)
    y_sample = _rmsnorm(xs, g_final)
    chunk_v_prompt = jnp.stack(vp_l)
    conv_prompt = jnp.stack(cp_l)
    chunk_v_sample = jnp.stack(vs_l)
    conv_sample = jnp.stack(cs_l)
    return (y_prompt, y_sample, chunk_v_prompt, conv_prompt, chunk_v_sample, conv_sample)
```

```python
import functools

import jax
import jax.numpy as jnp
from jax import lax
from jax.experimental import pallas as pl
from jax.experimental.pallas import tpu as pltpu

D_MODEL = 2048
W_A = 1024
W_B = 1024
HD = 128
N_GROUPS = W_A // HD
CHUNK = 128
CONV_W = 31
D_IN = 3 * W_A + 3 * W_B
EPS = 1e-6
INV_SQRT2 = 0.7071067811865476

COL_U, COL_V, COL_ZA, COL_A, COL_B, COL_ZB = 0, W_A, 2 * W_A, 3 * W_A, 3 * W_A + W_B, 3 * W_A + 2 * W_B

TM = 256
HIST = 32
CONV_ROWS = 64
VMEM_LIMIT_BYTES = 60 * 1024 * 1024

F32 = jnp.float32
BF16 = jnp.bfloat16


def _gelu(x):
    return 0.5 * x * (1.0 + lax.erf(x * INV_SQRT2))


def _silu(x):
    return x * jax.nn.sigmoid(x)


def _ln_lanes(x, g, b):
    mu = jnp.mean(x, axis=-1, keepdims=True)
    xc = x - mu
    var = jnp.mean(xc * xc, axis=-1, keepdims=True)
    return xc * lax.rsqrt(var + EPS) * g + b


def _rms(x, g):
    return x * lax.rsqrt(jnp.mean(x * x, axis=-1, keepdims=True) + EPS) * g


def _dot(a, b):
    return jnp.dot(a, b, preferred_element_type=F32)


def _ada_kernel(c_ref, w_ref, b_ref, o_ref):
    c = _silu(c_ref[...]).astype(BF16)
    o_ref[...] = _dot(c, w_ref[...].astype(BF16)) + b_ref[...]


def _ada_call(c_all, w_ada, b_ada):
    depth, d, n3 = w_ada.shape
    m = c_all.shape[0]
    bn = 512
    return pl.pallas_call(
        _ada_kernel,
        grid=(depth, n3 // bn),
        in_specs=[
            pl.BlockSpec((m, d), lambda l, j: (0, 0)),
            pl.BlockSpec((None, d, bn), lambda l, j: (l, 0, j)),
            pl.BlockSpec((None, 1, bn), lambda l, j: (l, 0, j)),
        ],
        out_specs=pl.BlockSpec((None, m, bn), lambda l, j: (l, 0, j)),
        out_shape=jax.ShapeDtypeStruct((depth, m, n3), F32),
        compiler_params=pltpu.CompilerParams(dimension_semantics=("arbitrary", "arbitrary")),
        name="ada_mod",
    )(c_all, w_ada, b_ada.reshape(depth, 1, n3))


def _prompt_kernel(x_ref, mod_ref, gn_ref, win_ref, wout_ref, ws_ref, bst_ref, gv_ref, bv_ref,
                   wc_ref, bc_ref, gcn_ref, bcn_ref, gfin_ref,
                   y_ref, vn_out_ref, glu_out_ref,
                   h_scr, ug_scr, vn_scr, za_scr, zb_scr, glu_scr, mix_scr, wsm_scr,
                   *, tiles_per_seq, final):
    t = pl.program_id(0)
    tm = x_ref.shape[0]
    n_chunks = tm // CHUNK

    @pl.when(t % tiles_per_seq == 0)
    def _():
        glu_scr[0:HIST, :] = jnp.zeros((HIST, W_B), F32)

    @pl.when(t == 0)
    def _():
        row = lax.broadcasted_iota(jnp.int32, (CHUNK, CHUNK), 0)
        col = lax.broadcasted_iota(jnp.int32, (CHUNK, CHUNK), 1)
        for h in range(N_GROUPS):
            wsm_scr[h] = jnp.where(row >= col, ws_ref[h], 0.0).astype(BF16)

    shift = mod_ref[:, 0:D_MODEL]
    scale = mod_ref[:, D_MODEL:2 * D_MODEL]
    gate = mod_ref[:, 2 * D_MODEL:3 * D_MODEL]

    h_scr[...] = (_rms(x_ref[...], gn_ref[...]) * (1.0 + scale) + shift).astype(BF16)
    hb = h_scr[...]

    def proj(c0, width=W_A):
        return _dot(hb, win_ref[:, c0:c0 + width])

    glu_scr[HIST:HIST + tm, :] = proj(COL_A) * jax.nn.sigmoid(proj(COL_B))
    zb_scr[...] = _silu(proj(COL_ZB))
    ug_scr[...] = _gelu(proj(COL_U))
    vg = _gelu(proj(COL_V))
    for h in range(N_GROUPS):
        hs = slice(h * HD, (h + 1) * HD)
        vn_scr[:, hs] = _ln_lanes(vg[:, hs], gv_ref[:, hs], bv_ref[:, hs])
    za_scr[...] = _silu(proj(COL_ZA))

    for g in range(N_GROUPS):
        gs = slice(g * HD, (g + 1) * HD)
        taps = [wc_ref[k:k + 1, gs] for k in range(CONV_W)]
        for rb in range(tm // CONV_ROWS):
            r0 = rb * CONV_ROWS
            acc = glu_scr[pl.ds(r0 + HIST - (CONV_W - 1), CONV_ROWS), gs] * taps[0]
            for k in range(1, CONV_W):
                acc = acc + glu_scr[pl.ds(r0 + HIST - (CONV_W - 1) + k, CONV_ROWS), gs] * taps[k]
            yb = _silu(_ln_lanes(acc + bc_ref[:, gs], gcn_ref[:, gs], bcn_ref[:, gs]))
            mix_scr[r0:r0 + CONV_ROWS, W_A + g * HD:W_A + (g + 1) * HD] = (
                yb * zb_scr[r0:r0 + CONV_ROWS, gs]).astype(BF16)

    for h in range(N_GROUPS):
        hs = slice(h * HD, (h + 1) * HD)
        v_all = jnp.concatenate(
            [vn_scr[c * CHUNK:(c + 1) * CHUNK, hs] for c in range(n_chunks)], axis=1).astype(BF16)
        s_all = _dot(wsm_scr[h], v_all)
        bias = bst_ref[:, h:h + 1]
        for c in range(n_chunks):
            rs = slice(c * CHUNK, (c + 1) * CHUNK)
            sp = s_all[:, c * CHUNK:(c + 1) * CHUNK] + bias
            mix_scr[rs, hs] = (ug_scr[rs, hs] * sp * za_scr[rs, hs]).astype(BF16)

    mix = mix_scr[...]
    half = D_MODEL // 2
    for n in range(2):
        ns = slice(n * half, (n + 1) * half)
        y_ref[:, ns] = x_ref[:, ns] + gate[:, ns] * _dot(mix, wout_ref[:, ns])
    if final:
        y_ref[...] = _rms(y_ref[...], gfin_ref[...])

    vn_out_ref[...] = vn_scr[tm - CHUNK:tm, :]
    glu_out_ref[...] = glu_scr[tm:tm + HIST, :]
    glu_scr[0:HIST, :] = glu_scr[tm:tm + HIST, :]


def _prompt_layer(x2d, mod_p, gn, win, wout, ws, bst, gv, bv, wc, bc, gcn, bcn, gfin, *, layer, seq, final):
    n_tok, d = x2d.shape
    n_seq = n_tok // seq
    tiles_per_seq = seq // TM
    row = lambda a: pl.BlockSpec((None, 1, a.shape[-1]), lambda t: (layer, 0, 0))
    resident = pl.Buffered(1)
    in_specs = [
        pl.BlockSpec((TM, d), lambda t: (t, 0)),
        pl.BlockSpec((None, 1, 3 * d), lambda t: (t // tiles_per_seq, 0, 0)),
        row(gn),
        pl.BlockSpec((None, d, D_IN), lambda t: (layer, 0, 0), pipeline_mode=resident),
        pl.BlockSpec((None, d, d), lambda t: (layer, 0, 0), pipeline_mode=resident),
        pl.BlockSpec((None, N_GROUPS, CHUNK, CHUNK), lambda t: (layer, 0, 0, 0)),
        pl.BlockSpec((None, CHUNK, N_GROUPS), lambda t: (layer, 0, 0)),
        row(gv), row(bv),
        pl.BlockSpec((None, CONV_W, W_B), lambda t: (layer, 0, 0)),
        row(bc), row(gcn), row(bcn),
        pl.BlockSpec((1, d), lambda t: (0, 0)),
    ]
    out_specs = [
        pl.BlockSpec((TM, d), lambda t: (t, 0)),
        pl.BlockSpec((None, CHUNK, W_A), lambda t: (t // tiles_per_seq, 0, 0)),
        pl.BlockSpec((None, HIST, W_B), lambda t: (t // tiles_per_seq, 0, 0)),
    ]
    out_shape = [
        jax.ShapeDtypeStruct((n_tok, d), F32),
        jax.ShapeDtypeStruct((n_seq, CHUNK, W_A), F32),
        jax.ShapeDtypeStruct((n_seq, HIST, W_B), F32),
    ]
    scratch = [
        pltpu.VMEM((TM, d), BF16),
        pltpu.VMEM((TM, W_A), F32),
        pltpu.VMEM((TM, W_A), F32),
        pltpu.VMEM((TM, W_A), F32),
        pltpu.VMEM((TM, W_B), F32),
        pltpu.VMEM((HIST + TM, W_B), F32),
        pltpu.VMEM((TM, W_A + W_B), BF16),
        pltpu.VMEM((N_GROUPS, CHUNK, CHUNK), BF16),
    ]
    return pl.pallas_call(
        functools.partial(_prompt_kernel, tiles_per_seq=tiles_per_seq, final=final),
        grid=(n_tok // TM,),
        in_specs=in_specs,
        out_specs=out_specs,
        out_shape=out_shape,
        scratch_shapes=scratch,
        compiler_params=pltpu.CompilerParams(
            dimension_semantics=("arbitrary",), vmem_limit_bytes=VMEM_LIMIT_BYTES),
        name=f"prompt_layer{layer}",
    )(x2d, mod_p, gn, win, wout, ws, bst, gv, bv, wc, bc, gcn, bcn, gfin)


def _state_kernel(s_ref, wc_ref, o_ref):
    o_ref[...] = jnp.sum(s_ref[...] * wc_ref[0:CONV_W - 1, :][None], axis=1)


def _state_call(state, w_conv):
    depth, n, rows, w = state.shape
    bs = 16
    return pl.pallas_call(
        _state_kernel,
        grid=(depth, n // bs),
        in_specs=[
            pl.BlockSpec((None, bs, rows, w), lambda l, i: (l, i, 0, 0)),
            pl.BlockSpec((None, CONV_W, w), lambda l, i: (l, 0, 0)),
        ],
        out_specs=pl.BlockSpec((None, bs, w), lambda l, i: (l, i, 0)),
        out_shape=jax.ShapeDtypeStruct((depth, n, w), F32),
        compiler_params=pltpu.CompilerParams(dimension_semantics=("arbitrary", "arbitrary")),
        name="sample_conv_state",
    )(state, w_conv)


def _sample_kernel(x_ref, mod_ref, gn_ref, win_ref, wout_ref, ws0_ref, bs0_ref, gv_ref, bv_ref,
                   cpart_ref, wlast_ref, bc_ref, gcn_ref, bcn_ref, gfin_ref,
                   y_ref, vn_out_ref, glu_out_ref, *, final):
    shift = mod_ref[:, 0:D_MODEL]
    scale = mod_ref[:, D_MODEL:2 * D_MODEL]
    gate = mod_ref[:, 2 * D_MODEL:3 * D_MODEL]
    x = x_ref[...]
    hb = (_rms(x, gn_ref[...]) * (1.0 + scale) + shift).astype(BF16)

    def proj(c0, width=W_A):
        return _dot(hb, win_ref[:, c0:c0 + width])

    glu = proj(COL_A) * jax.nn.sigmoid(proj(COL_B))
    glu_out_ref[...] = glu
    zb = _silu(proj(COL_ZB))
    ug = _gelu(proj(COL_U))
    vg = _gelu(proj(COL_V))
    za = _silu(proj(COL_ZA))
    conv = cpart_ref[...] + glu * wlast_ref[...] + bc_ref[...]
    mixed = []
    vns = []
    for h in range(N_GROUPS):
        hs = slice(h * HD, (h + 1) * HD)
        vn = _ln_lanes(vg[:, hs], gv_ref[:, hs], bv_ref[:, hs])
        vns.append(vn)
        sp = vn * ws0_ref[:, hs] + bs0_ref[:, hs]
        mixed.append((ug[:, hs] * sp * za[:, hs]).astype(BF16))
    vn_out_ref[...] = jnp.concatenate(vns, axis=1)
    for g in range(N_GROUPS):
        gs = slice(g * HD, (g + 1) * HD)
        yb = _silu(_ln_lanes(conv[:, gs], gcn_ref[:, gs], bcn_ref[:, gs]))
        mixed.append((yb * zb[:, gs]).astype(BF16))
    mix = jnp.concatenate(mixed, axis=1)
    y = x + gate * _dot(mix, wout_ref[...])
    if final:
        y = _rms(y, gfin_ref[...])
    y_ref[...] = y


def _sample_layer(x2d, mod_s, gn, win, wout, ws0, bs0, gv, bv, cpart, wlast, bc, gcn, bcn, gfin, *, layer, final):
    n, d = x2d.shape
    row = lambda a: pl.BlockSpec((None, 1, a.shape[-1]), lambda i: (layer, 0, 0))
    full = lambda a: pl.BlockSpec(a.shape, lambda i: (0,) * a.ndim)
    in_specs = [
        full(x2d), full(mod_s), row(gn),
        pl.BlockSpec((None, d, D_IN), lambda i: (layer, 0, 0), pipeline_mode=pl.Buffered(1)),
        pl.BlockSpec((None, d, d), lambda i: (layer, 0, 0), pipeline_mode=pl.Buffered(1)),
        row(ws0), row(bs0), row(gv), row(bv),
        pl.BlockSpec((None, n, W_B), lambda i: (layer, 0, 0)),
        row(wlast), row(bc), row(gcn), row(bcn),
        full(gfin),
    ]
    out_shape = [
        jax.ShapeDtypeStruct((n, d), F32),
        jax.ShapeDtypeStruct((n, W_A), F32),
        jax.ShapeDtypeStruct((n, W_B), F32),
    ]
    return pl.pallas_call(
        functools.partial(_sample_kernel, final=final),
        grid=(1,),
        in_specs=in_specs,
        out_specs=[full(s) for s in out_shape],
        out_shape=out_shape,
        compiler_params=pltpu.CompilerParams(vmem_limit_bytes=VMEM_LIMIT_BYTES),
        name=f"sample_layer{layer}",
    )(x2d, mod_s, gn, win, wout, ws0, bs0, gv, bv, cpart, wlast, bc, gcn, bcn, gfin)


def kernel(x_prompt, x_sample, c_prompt, c_sample, state_conv, g_norm, w_ada, b_ada, w_in, w_s, b_s, g_v, b_v, w_conv, b_conv, g_cn, b_cn, w_out, g_final):
    n_p, seq, d = x_prompt.shape
    n_s = x_sample.shape[0]
    depth = w_in.shape[0]
    assert seq % TM == 0 and TM % CHUNK == 0 and x_sample.shape[1] == 1

    r3 = lambda a: a.reshape(depth, 1, a.shape[-1])
    gn, gv, bv, bc, gcn, bcn = map(r3, (g_norm, g_v, b_v, b_conv, g_cn, b_cn))
    gfin = g_final.reshape(1, d)
    win = w_in.astype(BF16)
    wout = w_out.astype(BF16)
    bst = jnp.swapaxes(b_s, 1, 2)
    ws0 = r3(jnp.repeat(w_s[:, :, 0, 0], HD, axis=-1))
    bs0 = r3(jnp.repeat(b_s[:, :, 0], HD, axis=-1))
    wlast = r3(w_conv[:, CONV_W - 1, :])

    c_all = jnp.concatenate([c_prompt, c_sample], axis=0)
    pad = (-c_all.shape[0]) % 8
    c_all = jnp.pad(c_all, ((0, pad), (0, 0)))
    mod = _ada_call(c_all, w_ada, b_ada)
    cpart = _state_call(state_conv, w_conv)

    xp = x_prompt.reshape(n_p * seq, d)
    xs = x_sample.reshape(n_s, d)
    vp_l, cp_l, vs_l, cs_l = [], [], [], []
    for l in range(depth):
        final = l == depth - 1
        mod_p = mod[l, :n_p].reshape(n_p, 1, 3 * d)
        mod_s = mod[l, n_p:n_p + n_s]
        xp, vp, gp = _prompt_layer(xp, mod_p, gn, win, wout, w_s, bst, gv, bv, w_conv, bc, gcn, bcn, gfin,
                                   layer=l, seq=seq, final=final)
        xs, vs, gs = _sample_layer(xs, mod_s, gn, win, wout, ws0, bs0, gv, bv, cpart, wlast, bc, gcn, bcn, gfin,
                                   layer=l, final=final)
        vp_l.append(vp.reshape(n_p, CHUNK, N_GROUPS, HD))
        cp_l.append(gp[:, HIST - (CONV_W - 1):, :])
        vs_l.append(vs.reshape(n_s, 1, N_GROUPS, HD))
        cs_l.append(jnp.concatenate([state_conv[l][:, 1:, :], gs[:, None, :]], axis=1))
    return (xp.reshape(n_p, seq, d), xs.reshape(n_s, 1, d),
            jnp.stack(vp_l), jnp.stack(cp_l), jnp.stack(vs_l), jnp.stack(cs_l))
```

```python
import functools

import jax
import jax.numpy as jnp
from jax import lax
from jax.experimental import pallas as pl
from jax.experimental.pallas import tpu as pltpu

D_MODEL = 2048
W_A = 1024
W_B = 1024
HD = 128
N_GROUPS = W_A // HD
CHUNK = 128
CONV_W = 31
D_IN = 3 * W_A + 3 * W_B
EPS = 1e-6
INV_SQRT2 = 0.7071067811865476

BLK_U, BLK_V, BLK_ZA, BLK_A, BLK_B, BLK_ZB = range(6)
N_IN_BLK = D_IN // W_A
N_OUT_BLK = D_MODEL // W_A

TM = 256
HIST = 32
CONV_ROWS = 64
N_STAGE = 4
SAMPLE_BN = 512
VMEM_LIMIT_BYTES = 60 * 1024 * 1024

F32 = jnp.float32
BF16 = jnp.bfloat16


def _gelu(x):
    return 0.5 * x * (1.0 + lax.erf(x * INV_SQRT2))


def _silu(x):
    return x * jax.nn.sigmoid(x)


def _ln_lanes(x, g, b):
    mu = jnp.mean(x, axis=-1, keepdims=True)
    xc = x - mu
    var = jnp.mean(xc * xc, axis=-1, keepdims=True)
    return xc * lax.rsqrt(var + EPS) * g + b


def _rms(x, g):
    return x * lax.rsqrt(jnp.mean(x * x, axis=-1, keepdims=True) + EPS) * g


def _dot(a, b):
    return jnp.dot(a, b, preferred_element_type=F32)


def _ada_kernel(cs_ref, cp_ref, w_ref, b_ref, os_ref, op_ref):
    w = w_ref[...].astype(BF16)
    os_ref[...] = _dot(_silu(cs_ref[...]).astype(BF16), w) + b_ref[...]
    op_ref[...] = _dot(_silu(cp_ref[...]).astype(BF16), w) + b_ref[...]


def _ada_call(c_s, c_p, w_ada, b_ada):
    depth, d, n3 = w_ada.shape
    ns, npad = c_s.shape[0], c_p.shape[0]
    bn = 512
    return pl.pallas_call(
        _ada_kernel,
        grid=(depth, n3 // bn),
        in_specs=[
            pl.BlockSpec((ns, d), lambda l, j: (0, 0)),
            pl.BlockSpec((npad, d), lambda l, j: (0, 0)),
            pl.BlockSpec((None, d, bn), lambda l, j: (l, 0, j)),
            pl.BlockSpec((None, 1, bn), lambda l, j: (l, 0, j)),
        ],
        out_specs=[
            pl.BlockSpec((None, ns, bn), lambda l, j: (l, 0, j)),
            pl.BlockSpec((None, npad, bn), lambda l, j: (l, 0, j)),
        ],
        out_shape=[jax.ShapeDtypeStruct((depth, ns, n3), F32), jax.ShapeDtypeStruct((depth, npad, n3), F32)],
        compiler_params=pltpu.CompilerParams(dimension_semantics=("arbitrary", "arbitrary")),
        name="ada_mod",
    )(c_s, c_p, w_ada, b_ada.reshape(depth, 1, n3))


def _load_weight_blocks(w_hbm, layer, w_scr, blk0, n_blk, stage, sem):
    rows_per_blk = w_scr.shape[1] // TM
    n = n_blk * rows_per_blk

    def copy(i, slot):
        c = i // rows_per_blk
        r = i % rows_per_blk
        src = w_hbm.at[layer, pl.ds(pl.multiple_of(r * TM, TM), TM), pl.ds(pl.multiple_of(c * W_A, W_A), W_A)]
        return pltpu.make_async_copy(src, stage.at[slot], sem.at[slot])

    for i in range(N_STAGE - 1):
        copy(jnp.int32(i), i).start()

    def body(i, carry):
        slot = i % N_STAGE
        nxt = i + N_STAGE - 1

        @pl.when(nxt < n)
        def _():
            copy(nxt, nxt % N_STAGE).start()

        copy(i, slot).wait()
        c = i // rows_per_blk
        r = i % rows_per_blk
        w_scr[blk0 + c, pl.ds(pl.multiple_of(r * TM, TM), TM), :] = stage[slot].astype(BF16)
        return carry

    lax.fori_loop(0, n, body, 0)


def _prompt_kernel(x_ref, mod_ref, gn_ref, win_hbm, wout_hbm, ws_ref, bst_ref, gv_ref, bv_ref,
                   wc_ref, bc_ref, gcn_ref, bcn_ref, gfin_ref,
                   y_ref, vn_out_ref, glu_out_ref,
                   w_scr, act_scr, h_scr, glu_scr, mix_scr, wsm_scr, zero_scr, sem,
                   *, layer, tiles_per_seq, final):
    t = pl.program_id(0)
    tm = x_ref.shape[0]
    n_chunks = tm // CHUNK
    ug_scr, vn_scr, za_scr, zb_scr = (act_scr.at[i] for i in range(4))

    @pl.when(t == 0)
    def _():
        _load_weight_blocks(win_hbm, layer, w_scr, 0, N_IN_BLK, act_scr, sem)
        _load_weight_blocks(wout_hbm, layer, w_scr, N_IN_BLK, N_OUT_BLK, act_scr, sem)
        row = lax.broadcasted_iota(jnp.int32, (CHUNK, CHUNK), 0)
        col = lax.broadcasted_iota(jnp.int32, (CHUNK, CHUNK), 1)
        for h in range(N_GROUPS):
            wsm_scr[h] = jnp.where(row >= col, ws_ref[h], 0.0).astype(BF16)
        zero_scr[0] = 0

    @pl.when(t % tiles_per_seq == 0)
    def _():
        glu_scr[:, 0:HIST, :] = jnp.zeros((N_GROUPS, HIST, HD), F32)

    mrow = mod_ref[pl.ds(t // tiles_per_seq, 1), :]
    shift = mrow[:, 0:D_MODEL]
    scale = mrow[:, D_MODEL:2 * D_MODEL]
    gate = mrow[:, 2 * D_MODEL:3 * D_MODEL]

    h_scr[...] = (_rms(x_ref[...], gn_ref[...]) * (1.0 + scale) + shift).astype(BF16)
    hb = h_scr[...]

    def proj(blk):
        return _dot(hb, w_scr[blk])

    glu = proj(BLK_A) * jax.nn.sigmoid(proj(BLK_B))
    for g in range(N_GROUPS):
        glu_scr[g, HIST:HIST + tm, :] = glu[:, g * HD:(g + 1) * HD]
    zb_scr[...] = _silu(proj(BLK_ZB))
    ug_scr[...] = _gelu(proj(BLK_U))
    vg = _gelu(proj(BLK_V))
    for h in range(N_GROUPS):
        hs = slice(h * HD, (h + 1) * HD)
        vn_scr[:, hs] = _ln_lanes(vg[:, hs], gv_ref[:, hs], bv_ref[:, hs])
    za_scr[...] = _silu(proj(BLK_ZA))

    base = zero_scr[0] + (HIST - (CONV_W - 1))
    for g in range(N_GROUPS):
        gs = slice(g * HD, (g + 1) * HD)
        taps = [wc_ref[k:k + 1, gs] for k in range(CONV_W)]
        for rb in range(tm // CONV_ROWS):
            r0 = rb * CONV_ROWS
            acc = glu_scr[g, pl.ds(base + r0, CONV_ROWS), :] * taps[0]
            for k in range(1, CONV_W):
                acc = acc + glu_scr[g, pl.ds(base + (r0 + k), CONV_ROWS), :] * taps[k]
            yb = _silu(_ln_lanes(acc + bc_ref[:, gs], gcn_ref[:, gs], bcn_ref[:, gs]))
            mix_scr[r0:r0 + CONV_ROWS, W_A + g * HD:W_A + (g + 1) * HD] = (
                yb * zb_scr[r0:r0 + CONV_ROWS, gs]).astype(BF16)

    for h in range(N_GROUPS):
        hs = slice(h * HD, (h + 1) * HD)
        v_all = jnp.concatenate(
            [vn_scr[c * CHUNK:(c + 1) * CHUNK, hs] for c in range(n_chunks)], axis=1).astype(BF16)
        s_all = _dot(wsm_scr[h], v_all)
        bias = bst_ref[:, h:h + 1]
        for c in range(n_chunks):
            rs = slice(c * CHUNK, (c + 1) * CHUNK)
            sp = s_all[:, c * CHUNK:(c + 1) * CHUNK] + bias
            mix_scr[rs, hs] = (ug_scr[rs, hs] * sp * za_scr[rs, hs]).astype(BF16)

    mix = mix_scr[...]
    for n in range(N_OUT_BLK):
        ns = slice(n * W_A, (n + 1) * W_A)
        y_ref[:, ns] = x_ref[:, ns] + gate[:, ns] * _dot(mix, w_scr[N_IN_BLK + n])
    if final:
        y_ref[...] = _rms(y_ref[...], gfin_ref[...])

    vn_out_ref[...] = vn_scr[tm - CHUNK:tm, :]
    for g in range(N_GROUPS):
        tail = glu_scr[g, tm:tm + HIST, :]
        glu_out_ref[:, g * HD:(g + 1) * HD] = tail
        glu_scr[g, 0:HIST, :] = tail


def _prompt_layer(x2d, mod_p, gn, w_in, w_out, ws, bst, gv, bv, wc, bc, gcn, bcn, gfin, *, layer, seq, final):
    n_tok, d = x2d.shape
    n_seq = n_tok // seq
    tiles_per_seq = seq // TM
    row = lambda a: pl.BlockSpec((None, 1, a.shape[-1]), lambda t: (layer, 0, 0))
    in_specs = [
        pl.BlockSpec((TM, d), lambda t: (t, 0)),
        pl.BlockSpec((None,) + mod_p.shape[1:], lambda t: (layer, 0, 0)),
        row(gn),
        pl.BlockSpec(memory_space=pl.ANY),
        pl.BlockSpec(memory_space=pl.ANY),
        pl.BlockSpec((None, N_GROUPS, CHUNK, CHUNK), lambda t: (layer, 0, 0, 0)),
        pl.BlockSpec((None, CHUNK, N_GROUPS), lambda t: (layer, 0, 0)),
        row(gv), row(bv),
        pl.BlockSpec((None, CONV_W, W_B), lambda t: (layer, 0, 0)),
        row(bc), row(gcn), row(bcn),
        pl.BlockSpec((1, d), lambda t: (0, 0)),
    ]
    out_specs = [
        pl.BlockSpec((TM, d), lambda t: (t, 0)),
        pl.BlockSpec((None, CHUNK, W_A), lambda t: (t // tiles_per_seq, 0, 0)),
        pl.BlockSpec((None, HIST, W_B), lambda t: (t // tiles_per_seq, 0, 0)),
    ]
    out_shape = [
        jax.ShapeDtypeStruct((n_tok, d), F32),
        jax.ShapeDtypeStruct((n_seq, CHUNK, W_A), F32),
        jax.ShapeDtypeStruct((n_seq, HIST, W_B), F32),
    ]
    scratch = [
        pltpu.VMEM((N_IN_BLK + N_OUT_BLK, d, W_A), BF16),
        pltpu.VMEM((N_STAGE, TM, W_A), F32),
        pltpu.VMEM((TM, d), BF16),
        pltpu.VMEM((N_GROUPS, HIST + TM, HD), F32),
        pltpu.VMEM((TM, W_A + W_B), BF16),
        pltpu.VMEM((N_GROUPS, CHUNK, CHUNK), BF16),
        pltpu.SMEM((1,), jnp.int32),
        pltpu.SemaphoreType.DMA((N_STAGE,)),
    ]
    return pl.pallas_call(
        functools.partial(_prompt_kernel, layer=layer, tiles_per_seq=tiles_per_seq, final=final),
        grid=(n_tok // TM,),
        in_specs=in_specs,
        out_specs=out_specs,
        out_shape=out_shape,
        scratch_shapes=scratch,
        compiler_params=pltpu.CompilerParams(
            dimension_semantics=("arbitrary",), vmem_limit_bytes=VMEM_LIMIT_BYTES),
        name=f"prompt_layer{layer}",
    )(x2d, mod_p, gn, w_in, w_out, ws, bst, gv, bv, wc, bc, gcn, bcn, gfin)


def _state_dot_kernel(s_ref, wc_ref, o_ref):
    o_ref[...] = jnp.sum(s_ref[...] * wc_ref[0:CONV_W - 1, :][None], axis=1)


def _state_shift_kernel(s_ref, g_ref, o_ref):
    n_rows = s_ref.shape[1]
    o_ref[:, 0:n_rows - 1, :] = s_ref[:, 1:n_rows, :]
    for i in range(s_ref.shape[0]):
        o_ref[i, n_rows - 1:n_rows, :] = g_ref[i:i + 1, :]


def _state_dot_call(state, w_conv):
    depth, n, rows, w = state.shape
    bs = 16
    return pl.pallas_call(
        _state_dot_kernel,
        grid=(depth, n // bs),
        in_specs=[
            pl.BlockSpec((None, bs, rows, w), lambda l, i: (l, i, 0, 0)),
            pl.BlockSpec((None, CONV_W, w), lambda l, i: (l, 0, 0)),
        ],
        out_specs=pl.BlockSpec((None, bs, w), lambda l, i: (l, i, 0)),
        out_shape=jax.ShapeDtypeStruct((depth, n, w), F32),
        compiler_params=pltpu.CompilerParams(dimension_semantics=("arbitrary", "arbitrary")),
        name="sample_conv_state",
    )(state, w_conv)


def _state_shift_call(state, glu):
    depth, n, rows, w = state.shape
    bs = 16
    return pl.pallas_call(
        _state_shift_kernel,
        grid=(depth, n // bs),
        in_specs=[
            pl.BlockSpec((None, bs, rows, w), lambda l, i: (l, i, 0, 0)),
            pl.BlockSpec((None, bs, w), lambda l, i: (l, i, 0)),
        ],
        out_specs=pl.BlockSpec((None, bs, rows, w), lambda l, i: (l, i, 0, 0)),
        out_shape=jax.ShapeDtypeStruct(state.shape, F32),
        compiler_params=pltpu.CompilerParams(dimension_semantics=("arbitrary", "arbitrary")),
        name="sample_state_shift",
    )(state, glu)


def _sample_kernel(x_ref, mod_ref, gn_ref, win_ref, wout_ref, ws0_ref, bs0_ref, gv_ref, bv_ref,
                   cpart_ref, wlast_ref, bc_ref, gcn_ref, bcn_ref, gfin_ref,
                   y_ref, vn_out_ref, glu_out_ref,
                   h_scr, proj_scr, mix_scr, y_scr, *, n_in_steps, n_out_steps, final):
    j = pl.program_id(0)
    per_blk = W_A // SAMPLE_BN

    @pl.when(j == 0)
    def _():
        shift = mod_ref[:, 0:D_MODEL]
        scale = mod_ref[:, D_MODEL:2 * D_MODEL]
        h_scr[...] = (_rms(x_ref[...], gn_ref[...]) * (1.0 + scale) + shift).astype(BF16)

    @pl.when(j < n_in_steps)
    def _():
        proj_scr[j] = _dot(h_scr[...], win_ref[...].astype(BF16))

    @pl.when(j == n_in_steps)
    def _():
        def blk(b):
            return jnp.concatenate([proj_scr[b * per_blk + i] for i in range(per_blk)], axis=1)

        glu = blk(BLK_A) * jax.nn.sigmoid(blk(BLK_B))
        glu_out_ref[...] = glu
        zb = _silu(blk(BLK_ZB))
        ug = _gelu(blk(BLK_U))
        vg = _gelu(blk(BLK_V))
        za = _silu(blk(BLK_ZA))
        conv = cpart_ref[...] + glu * wlast_ref[...] + bc_ref[...]
        for h in range(N_GROUPS):
            hs = slice(h * HD, (h + 1) * HD)
            vn = _ln_lanes(vg[:, hs], gv_ref[:, hs], bv_ref[:, hs])
            vn_out_ref[:, hs] = vn
            sp = vn * ws0_ref[:, hs] + bs0_ref[:, hs]
            mix_scr[:, hs] = (ug[:, hs] * sp * za[:, hs]).astype(BF16)
            yb = _silu(_ln_lanes(conv[:, hs], gcn_ref[:, hs], bcn_ref[:, hs]))
            mix_scr[:, W_A + h * HD:W_A + (h + 1) * HD] = (yb * zb[:, hs]).astype(BF16)

    @pl.when(j >= n_in_steps)
    def _():
        y_scr[j - n_in_steps] = _dot(mix_scr[...], wout_ref[...].astype(BF16))

    @pl.when(j == n_in_steps + n_out_steps - 1)
    def _():
        gate = mod_ref[:, 2 * D_MODEL:3 * D_MODEL]
        y = x_ref[...] + gate * jnp.concatenate([y_scr[i] for i in range(n_out_steps)], axis=1)
        if final:
            y = _rms(y, gfin_ref[...])
        y_ref[...] = y


def _sample_layer(x2d, mod_s, gn, w_in, w_out, ws0, bs0, gv, bv, cpart, wlast, bc, gcn, bcn, gfin, *, layer, final):
    n, d = x2d.shape
    n_in_steps = D_IN // SAMPLE_BN
    n_out_steps = d // SAMPLE_BN
    row = lambda a: pl.BlockSpec((None, 1, a.shape[-1]), lambda j: (layer, 0, 0))
    full = lambda a: pl.BlockSpec(a.shape, lambda j: (0,) * a.ndim)
    in_specs = [
        full(x2d),
        pl.BlockSpec((None, n, 3 * d), lambda j: (layer, 0, 0)),
        row(gn),
        pl.BlockSpec((None, d, SAMPLE_BN), lambda j: (layer, 0, jnp.minimum(j, n_in_steps - 1))),
        pl.BlockSpec((None, d, SAMPLE_BN), lambda j: (layer, 0, jnp.maximum(j - n_in_steps, 0))),
        row(ws0), row(bs0), row(gv), row(bv),
        pl.BlockSpec((None, n, W_B), lambda j: (layer, 0, 0)),
        row(wlast), row(bc), row(gcn), row(bcn),
        full(gfin),
    ]
    out_shape = [
        jax.ShapeDtypeStruct((n, d), F32),
        jax.ShapeDtypeStruct((n, W_A), F32),
        jax.ShapeDtypeStruct((n, W_B), F32),
    ]
    scratch = [
        pltpu.VMEM((n, d), BF16),
        pltpu.VMEM((n_in_steps, n, SAMPLE_BN), F32),
        pltpu.VMEM((n, W_A + W_B), BF16),
        pltpu.VMEM((n_out_steps, n, SAMPLE_BN), F32),
    ]
    return pl.pallas_call(
        functools.partial(_sample_kernel, n_in_steps=n_in_steps, n_out_steps=n_out_steps, final=final),
        grid=(n_in_steps + n_out_steps,),
        in_specs=in_specs,
        out_specs=[full(s) for s in out_shape],
        out_shape=out_shape,
        scratch_shapes=scratch,
        compiler_params=pltpu.CompilerParams(dimension_semantics=("arbitrary",)),
        name=f"sample_layer{layer}",
    )(x2d, mod_s, gn, w_in, w_out, ws0, bs0, gv, bv, cpart, wlast, bc, gcn, bcn, gfin)


def kernel(x_prompt, x_sample, c_prompt, c_sample, state_conv, g_norm, w_ada, b_ada, w_in, w_s, b_s, g_v, b_v, w_conv, b_conv, g_cn, b_cn, w_out, g_final):
    n_p, seq, d = x_prompt.shape
    n_s = x_sample.shape[0]
    depth = w_in.shape[0]
    assert seq % TM == 0 and TM % CHUNK == 0 and x_sample.shape[1] == 1
    assert d == D_MODEL and w_in.shape[2] == D_IN

    r3 = lambda a: a.reshape(depth, 1, a.shape[-1])
    gn, gv, bv, bc, gcn, bcn = map(r3, (g_norm, g_v, b_v, b_conv, g_cn, b_cn))
    gfin = g_final.reshape(1, d)
    bst = jnp.swapaxes(b_s, 1, 2)
    ws0 = r3(jnp.repeat(w_s[:, :, 0, 0], HD, axis=-1))
    bs0 = r3(jnp.repeat(b_s[:, :, 0], HD, axis=-1))
    wlast = r3(w_conv[:, CONV_W - 1, :])

    c_p = jnp.pad(c_prompt, ((0, (-n_p) % 8), (0, 0)))
    mod_s, mod_p = _ada_call(c_sample, c_p, w_ada, b_ada)
    cpart = _state_dot_call(state_conv, w_conv)

    xp = x_prompt.reshape(n_p * seq, d)
    xs = x_sample.reshape(n_s, d)
    vp_l, cp_l, vs_l, gs_l = [], [], [], []
    for l in range(depth):
        final = l == depth - 1
        xp, vp, gp = _prompt_layer(xp, mod_p, gn, w_in, w_out, w_s, bst, gv, bv, w_conv, bc, gcn, bcn, gfin,
                                   layer=l, seq=seq, final=final)
        xs, vs, gs = _sample_layer(xs, mod_s, gn, w_in, w_out, ws0, bs0, gv, bv, cpart, wlast, bc, gcn, bcn, gfin,
                                   layer=l, final=final)
        vp_l.append(vp)
        cp_l.append(gp[:, HIST - (CONV_W - 1):, :])
        vs_l.append(vs)
        gs_l.append(gs)
    conv_sample = _state_shift_call(state_conv, jnp.stack(gs_l))
    return (xp.reshape(n_p, seq, d), xs.reshape(n_s, 1, d),
            jnp.stack(vp_l).reshape(depth, n_p, CHUNK, N_GROUPS, HD), jnp.stack(cp_l),
            jnp.stack(vs_l).reshape(depth, n_s, 1, N_GROUPS, HD), conv_sample)
```

```python
import functools

import jax
import jax.numpy as jnp
from jax import lax
from jax.experimental import pallas as pl
from jax.experimental.pallas import tpu as pltpu

D_MODEL = 2048
W_A = 1024
W_B = 1024
HD = 128
N_GROUPS = W_A // HD
CHUNK = 128
CONV_W = 31
D_IN = 3 * W_A + 3 * W_B
EPS = 1e-6
INV_SQRT2 = 0.7071067811865476

BLK_U, BLK_V, BLK_ZA, BLK_A, BLK_B, BLK_ZB = range(6)
N_IN_BLK = D_IN // W_A
N_OUT_BLK = D_MODEL // W_A

TM = 256
HIST = 32
CONV_ROWS = 64
N_STAGE = 4
SAMPLE_BN = 512
VMEM_LIMIT_BYTES = 60 * 1024 * 1024

F32 = jnp.float32
BF16 = jnp.bfloat16


def _gelu(x):
    return 0.5 * x * (1.0 + lax.erf(x * INV_SQRT2))


def _silu(x):
    return x * jax.nn.sigmoid(x)


def _ln_lanes(x, g, b):
    mu = jnp.mean(x, axis=-1, keepdims=True)
    xc = x - mu
    var = jnp.mean(xc * xc, axis=-1, keepdims=True)
    return xc * lax.rsqrt(var + EPS) * g + b


def _rms(x, g):
    return x * lax.rsqrt(jnp.mean(x * x, axis=-1, keepdims=True) + EPS) * g


def _dot(a, b):
    return jnp.dot(a, b, preferred_element_type=F32)


def _ada_kernel(cs_ref, cp_ref, w_ref, b_ref, os_ref, op_ref):
    w = w_ref[...].astype(BF16)
    os_ref[...] = _dot(_silu(cs_ref[...]).astype(BF16), w) + b_ref[...]
    op_ref[...] = _dot(_silu(cp_ref[...]).astype(BF16), w) + b_ref[...]


def _ada_call(c_s, c_p, w_ada, b_ada):
    depth, d, n3 = w_ada.shape
    ns, npad = c_s.shape[0], c_p.shape[0]
    bn = 512
    return pl.pallas_call(
        _ada_kernel,
        grid=(depth, n3 // bn),
        in_specs=[
            pl.BlockSpec((ns, d), lambda l, j: (0, 0)),
            pl.BlockSpec((npad, d), lambda l, j: (0, 0)),
            pl.BlockSpec((None, d, bn), lambda l, j: (l, 0, j)),
            pl.BlockSpec((None, 1, bn), lambda l, j: (l, 0, j)),
        ],
        out_specs=[
            pl.BlockSpec((None, ns, bn), lambda l, j: (l, 0, j)),
            pl.BlockSpec((None, npad, bn), lambda l, j: (l, 0, j)),
        ],
        out_shape=[jax.ShapeDtypeStruct((depth, ns, n3), F32), jax.ShapeDtypeStruct((depth, npad, n3), F32)],
        compiler_params=pltpu.CompilerParams(dimension_semantics=("arbitrary", "arbitrary")),
        name="ada_mod",
    )(c_s, c_p, w_ada, b_ada.reshape(depth, 1, n3))


def _load_weight_blocks(w_hbm, layer, w_scr, blk0, n_blk, stage, sem):
    rows_per_blk = w_scr.shape[1] // TM
    n = n_blk * rows_per_blk

    def copy(i, slot):
        c = i // rows_per_blk
        r = i % rows_per_blk
        src = w_hbm.at[layer, pl.ds(pl.multiple_of(r * TM, TM), TM), pl.ds(pl.multiple_of(c * W_A, W_A), W_A)]
        return pltpu.make_async_copy(src, stage.at[slot], sem.at[slot])

    for i in range(N_STAGE - 1):
        copy(jnp.int32(i), i).start()

    def body(i, carry):
        slot = i % N_STAGE
        nxt = i + N_STAGE - 1

        @pl.when(nxt < n)
        def _():
            copy(nxt, nxt % N_STAGE).start()

        copy(i, slot).wait()
        c = i // rows_per_blk
        r = i % rows_per_blk
        w_scr[blk0 + c, pl.ds(pl.multiple_of(r * TM, TM), TM), :] = stage[slot].astype(BF16)
        return carry

    lax.fori_loop(0, n, body, 0)


def _prompt_kernel(x_ref, mod_ref, gn_ref, win_hbm, wout_hbm, ws_ref, bst_ref, gv_ref, bv_ref,
                   wc_ref, bc_ref, gcn_ref, bcn_ref, gfin_ref,
                   y_ref, vn_out_ref, glu_out_ref,
                   w_scr, act_scr, h_scr, glu_scr, mix_scr, wsm_scr, tap_scr, zero_scr, sem,
                   *, layer, tiles_per_seq, final):
    t = pl.program_id(0)
    tm = x_ref.shape[0]
    n_chunks = tm // CHUNK
    ug_scr, vn_scr, za_scr, zb_scr = (act_scr.at[i] for i in range(4))

    @pl.when(t == 0)
    def _():
        _load_weight_blocks(win_hbm, layer, w_scr, 0, N_IN_BLK, act_scr, sem)
        _load_weight_blocks(wout_hbm, layer, w_scr, N_IN_BLK, N_OUT_BLK, act_scr, sem)
        row = lax.broadcasted_iota(jnp.int32, (CHUNK, CHUNK), 0)
        col = lax.broadcasted_iota(jnp.int32, (CHUNK, CHUNK), 1)
        for h in range(N_GROUPS):
            wsm_scr[h] = jnp.where(row >= col, ws_ref[h], 0.0).astype(BF16)
        for k in range(CONV_W):
            tap_scr[k] = jnp.broadcast_to(wc_ref[k:k + 1, :], (8, W_B))
        zero_scr[0] = 0

    @pl.when(t % tiles_per_seq == 0)
    def _():
        glu_scr[:, 0:HIST, :] = jnp.zeros((N_GROUPS, HIST, HD), F32)

    mrow = mod_ref[pl.ds(t // tiles_per_seq, 1), :]
    shift = mrow[:, 0:D_MODEL]
    scale = mrow[:, D_MODEL:2 * D_MODEL]
    gate = mrow[:, 2 * D_MODEL:3 * D_MODEL]

    h_scr[...] = (_rms(x_ref[...], gn_ref[...]) * (1.0 + scale) + shift).astype(BF16)
    hb = h_scr[...]

    quarter = 2 * HD
    n_quarters = W_A // quarter

    def proj(blk, q):
        return _dot(hb, w_scr[blk, :, q * quarter:(q + 1) * quarter])

    base = zero_scr[0] + (HIST - (CONV_W - 1))
    n_acc = CONV_ROWS // 8

    def conv_chunk(g, rb):
        gs = slice(g * HD, (g + 1) * HD)
        r0 = rb * CONV_ROWS
        acc = [None] * n_acc
        for j in range(CONV_W + 8 * (n_acc - 1)):
            window = glu_scr[g, pl.ds(base + (r0 + j), 8), :]
            for m in range(n_acc):
                k = j - 8 * m
                if 0 <= k < CONV_W:
                    term = window * tap_scr[k, :, gs]
                    acc[m] = term if acc[m] is None else acc[m] + term
        conv = jnp.concatenate(acc, axis=0) + bc_ref[:, gs]
        yb = _silu(_ln_lanes(conv, gcn_ref[:, gs], bcn_ref[:, gs]))
        mix_scr[r0:r0 + CONV_ROWS, W_A + g * HD:W_A + (g + 1) * HD] = (
            yb * zb_scr[r0:r0 + CONV_ROWS, gs]).astype(BF16)

    conv_chunks = [(g, rb) for g in range(N_GROUPS) for rb in range(tm // CONV_ROWS)]

    def emit_conv(n):
        for _ in range(n):
            conv_chunk(*conv_chunks.pop(0))

    def zb_piece(q):
        zb_scr[:, q * quarter:(q + 1) * quarter] = _silu(proj(BLK_ZB, q))

    def glu_piece(q, between):
        a = proj(BLK_A, q)
        emit_conv(between)
        glu = a * jax.nn.sigmoid(proj(BLK_B, q))
        for i in range(2):
            glu_scr[2 * q + i, HIST:HIST + tm, :] = glu[:, i * HD:(i + 1) * HD]

    zb_piece(0)
    glu_piece(0, 0)
    for q in range(1, n_quarters):
        zb_piece(q)
        emit_conv(2)
        glu_piece(q, 2)
        emit_conv(2)
    for q in range(n_quarters):
        ug_scr[:, q * quarter:(q + 1) * quarter] = _gelu(proj(BLK_U, q))
        emit_conv(2)
    for q in range(n_quarters):
        vg = _gelu(proj(BLK_V, q))
        for i in range(2):
            hs = slice((2 * q + i) * HD, (2 * q + i + 1) * HD)
            vn_scr[:, hs] = _ln_lanes(vg[:, i * HD:(i + 1) * HD], gv_ref[:, hs], bv_ref[:, hs])
        emit_conv(1)
    for q in range(n_quarters):
        za_scr[:, q * quarter:(q + 1) * quarter] = _silu(proj(BLK_ZA, q))
        emit_conv(1 if q < 2 else 0)
        for h in (2 * q, 2 * q + 1):
            hs = slice(h * HD, (h + 1) * HD)
            v_all = jnp.concatenate(
                [vn_scr[c * CHUNK:(c + 1) * CHUNK, hs] for c in range(n_chunks)], axis=1).astype(BF16)
            s_all = _dot(wsm_scr[h], v_all)
            bias = bst_ref[:, h:h + 1]
            for c in range(n_chunks):
                rs = slice(c * CHUNK, (c + 1) * CHUNK)
                sp = s_all[:, c * CHUNK:(c + 1) * CHUNK] + bias
                mix_scr[rs, hs] = (ug_scr[rs, hs] * sp * za_scr[rs, hs]).astype(BF16)
    assert not conv_chunks

    mix = mix_scr[...]
    for p in range(D_MODEL // quarter):
        ps = slice(p * quarter, (p + 1) * quarter)
        blk, q = divmod(p, n_quarters)
        mixed = _dot(mix, w_scr[N_IN_BLK + blk, :, q * quarter:(q + 1) * quarter])
        y_ref[:, ps] = x_ref[:, ps] + gate[:, ps] * mixed
    if final:
        y_ref[...] = _rms(y_ref[...], gfin_ref[...])

    vn_out_ref[...] = vn_scr[tm - CHUNK:tm, :]
    for g in range(N_GROUPS):
        tail = glu_scr[g, tm:tm + HIST, :]
        glu_out_ref[:, g * HD:(g + 1) * HD] = tail
        glu_scr[g, 0:HIST, :] = tail


def _prompt_layer(x2d, mod_p, gn, w_in, w_out, ws, bst, gv, bv, wc, bc, gcn, bcn, gfin, *, layer, seq, final):
    n_tok, d = x2d.shape
    n_seq = n_tok // seq
    tiles_per_seq = seq // TM
    row = lambda a: pl.BlockSpec((None, 1, a.shape[-1]), lambda t: (layer, 0, 0))
    in_specs = [
        pl.BlockSpec((TM, d), lambda t: (t, 0)),
        pl.BlockSpec((None,) + mod_p.shape[1:], lambda t: (layer, 0, 0)),
        row(gn),
        pl.BlockSpec(memory_space=pl.ANY),
        pl.BlockSpec(memory_space=pl.ANY),
        pl.BlockSpec((None, N_GROUPS, CHUNK, CHUNK), lambda t: (layer, 0, 0, 0)),
        pl.BlockSpec((None, CHUNK, N_GROUPS), lambda t: (layer, 0, 0)),
        row(gv), row(bv),
        pl.BlockSpec((None, CONV_W, W_B), lambda t: (layer, 0, 0)),
        row(bc), row(gcn), row(bcn),
        pl.BlockSpec((1, d), lambda t: (0, 0)),
    ]
    out_specs = [
        pl.BlockSpec((TM, d), lambda t: (t, 0)),
        pl.BlockSpec((None, CHUNK, W_A), lambda t: (t // tiles_per_seq, 0, 0)),
        pl.BlockSpec((None, HIST, W_B), lambda t: (t // tiles_per_seq, 0, 0)),
    ]
    out_shape = [
        jax.ShapeDtypeStruct((n_tok, d), F32),
        jax.ShapeDtypeStruct((n_seq, CHUNK, W_A), F32),
        jax.ShapeDtypeStruct((n_seq, HIST, W_B), F32),
    ]
    scratch = [
        pltpu.VMEM((N_IN_BLK + N_OUT_BLK, d, W_A), BF16),
        pltpu.VMEM((N_STAGE, TM, W_A), F32),
        pltpu.VMEM((TM, d), BF16),
        pltpu.VMEM((N_GROUPS, HIST + TM, HD), F32),
        pltpu.VMEM((TM, W_A + W_B), BF16),
        pltpu.VMEM((N_GROUPS, CHUNK, CHUNK), BF16),
        pltpu.VMEM((CONV_W, 8, W_B), F32),
        pltpu.SMEM((1,), jnp.int32),
        pltpu.SemaphoreType.DMA((N_STAGE,)),
    ]
    return pl.pallas_call(
        functools.partial(_prompt_kernel, layer=layer, tiles_per_seq=tiles_per_seq, final=final),
        grid=(n_tok // TM,),
        in_specs=in_specs,
        out_specs=out_specs,
        out_shape=out_shape,
        scratch_shapes=scratch,
        compiler_params=pltpu.CompilerParams(
            dimension_semantics=("arbitrary",), vmem_limit_bytes=VMEM_LIMIT_BYTES),
        name=f"prompt_layer{layer}",
    )(x2d, mod_p, gn, w_in, w_out, ws, bst, gv, bv, wc, bc, gcn, bcn, gfin)


STATE_BN = 256


def _state_dot_kernel(s_ref, wc_ref, o_ref):
    acc = s_ref[0] * wc_ref[0:1, :]
    for k in range(1, CONV_W - 1):
        acc = acc + s_ref[k] * wc_ref[k:k + 1, :]
    o_ref[...] = acc


def _state_shift_kernel(s_ref, g_ref, o_ref):
    n_rows = s_ref.shape[0]
    o_ref[0:n_rows - 1] = s_ref[1:n_rows]
    o_ref[n_rows - 1] = g_ref[...]


def _state_dot_call(state_t, w_conv):
    depth, rows, n, w = state_t.shape
    return pl.pallas_call(
        _state_dot_kernel,
        grid=(depth, w // STATE_BN),
        in_specs=[
            pl.BlockSpec((None, rows, n, STATE_BN), lambda l, j: (l, 0, 0, j)),
            pl.BlockSpec((None, CONV_W, STATE_BN), lambda l, j: (l, 0, j)),
        ],
        out_specs=pl.BlockSpec((None, n, STATE_BN), lambda l, j: (l, 0, j)),
        out_shape=jax.ShapeDtypeStruct((depth, n, w), F32),
        compiler_params=pltpu.CompilerParams(dimension_semantics=("arbitrary", "arbitrary")),
        name="sample_conv_state",
    )(state_t, w_conv)


def _state_shift_call(state_t, glu):
    depth, rows, n, w = state_t.shape
    return pl.pallas_call(
        _state_shift_kernel,
        grid=(depth, w // STATE_BN),
        in_specs=[
            pl.BlockSpec((None, rows, n, STATE_BN), lambda l, j: (l, 0, 0, j)),
            pl.BlockSpec((None, n, STATE_BN), lambda l, j: (l, 0, j)),
        ],
        out_specs=pl.BlockSpec((None, rows, n, STATE_BN), lambda l, j: (l, 0, 0, j)),
        out_shape=jax.ShapeDtypeStruct(state_t.shape, F32),
        compiler_params=pltpu.CompilerParams(dimension_semantics=("arbitrary", "arbitrary")),
        name="sample_state_shift",
    )(state_t, glu)


def _sample_kernel(x_ref, mod_ref, gn_ref, win_ref, wout_ref, ws0_ref, bs0_ref, gv_ref, bv_ref,
                   cpart_ref, wlast_ref, bc_ref, gcn_ref, bcn_ref, gfin_ref,
                   y_ref, vn_out_ref, glu_out_ref,
                   h_scr, proj_scr, mix_scr, y_scr, *, n_in_steps, n_out_steps, final):
    j = pl.program_id(0)
    per_blk = W_A // SAMPLE_BN

    @pl.when(j == 0)
    def _():
        shift = mod_ref[:, 0:D_MODEL]
        scale = mod_ref[:, D_MODEL:2 * D_MODEL]
        h_scr[...] = (_rms(x_ref[...], gn_ref[...]) * (1.0 + scale) + shift).astype(BF16)

    @pl.when(j < n_in_steps)
    def _():
        proj_scr[j] = _dot(h_scr[...], win_ref[...].astype(BF16))

    @pl.when(j == n_in_steps)
    def _():
        def blk(b):
            return jnp.concatenate([proj_scr[b * per_blk + i] for i in range(per_blk)], axis=1)

        glu = blk(BLK_A) * jax.nn.sigmoid(blk(BLK_B))
        glu_out_ref[...] = glu
        zb = _silu(blk(BLK_ZB))
        ug = _gelu(blk(BLK_U))
        vg = _gelu(blk(BLK_V))
        za = _silu(blk(BLK_ZA))
        conv = cpart_ref[...] + glu * wlast_ref[...] + bc_ref[...]
        for h in range(N_GROUPS):
            hs = slice(h * HD, (h + 1) * HD)
            vn = _ln_lanes(vg[:, hs], gv_ref[:, hs], bv_ref[:, hs])
            vn_out_ref[:, hs] = vn
            sp = vn * ws0_ref[:, hs] + bs0_ref[:, hs]
            mix_scr[:, hs] = (ug[:, hs] * sp * za[:, hs]).astype(BF16)
            yb = _silu(_ln_lanes(conv[:, hs], gcn_ref[:, hs], bcn_ref[:, hs]))
            mix_scr[:, W_A + h * HD:W_A + (h + 1) * HD] = (yb * zb[:, hs]).astype(BF16)

    @pl.when(j >= n_in_steps)
    def _():
        y_scr[j - n_in_steps] = _dot(mix_scr[...], wout_ref[...].astype(BF16))

    @pl.when(j == n_in_steps + n_out_steps - 1)
    def _():
        gate = mod_ref[:, 2 * D_MODEL:3 * D_MODEL]
        y = x_ref[...] + gate * jnp.concatenate([y_scr[i] for i in range(n_out_steps)], axis=1)
        if final:
            y = _rms(y, gfin_ref[...])
        y_ref[...] = y


def _sample_layer(x2d, mod_s, gn, w_in, w_out, ws0, bs0, gv, bv, cpart, wlast, bc, gcn, bcn, gfin, *, layer, final):
    n, d = x2d.shape
    n_in_steps = D_IN // SAMPLE_BN
    n_out_steps = d // SAMPLE_BN
    row = lambda a: pl.BlockSpec((None, 1, a.shape[-1]), lambda j: (layer, 0, 0))
    full = lambda a: pl.BlockSpec(a.shape, lambda j: (0,) * a.ndim)
    in_specs = [
        full(x2d),
        pl.BlockSpec((None, n, 3 * d), lambda j: (layer, 0, 0)),
        row(gn),
        pl.BlockSpec((None, d, SAMPLE_BN), lambda j: (layer, 0, jnp.minimum(j, n_in_steps - 1))),
        pl.BlockSpec((None, d, SAMPLE_BN), lambda j: (layer, 0, jnp.maximum(j - n_in_steps, 0))),
        row(ws0), row(bs0), row(gv), row(bv),
        pl.BlockSpec((None, n, W_B), lambda j: (layer, 0, 0)),
        row(wlast), row(bc), row(gcn), row(bcn),
        full(gfin),
    ]
    out_shape = [
        jax.ShapeDtypeStruct((n, d), F32),
        jax.ShapeDtypeStruct((n, W_A), F32),
        jax.ShapeDtypeStruct((n, W_B), F32),
    ]
    scratch = [
        pltpu.VMEM((n, d), BF16),
        pltpu.VMEM((n_in_steps, n, SAMPLE_BN), F32),
        pltpu.VMEM((n, W_A + W_B), BF16),
        pltpu.VMEM((n_out_steps, n, SAMPLE_BN), F32),
    ]
    return pl.pallas_call(
        functools.partial(_sample_kernel, n_in_steps=n_in_steps, n_out_steps=n_out_steps, final=final),
        grid=(n_in_steps + n_out_steps,),
        in_specs=in_specs,
        out_specs=[full(s) for s in out_shape],
        out_shape=out_shape,
        scratch_shapes=scratch,
        compiler_params=pltpu.CompilerParams(dimension_semantics=("arbitrary",)),
        name=f"sample_layer{layer}",
    )(x2d, mod_s, gn, w_in, w_out, ws0, bs0, gv, bv, cpart, wlast, bc, gcn, bcn, gfin)


def kernel(x_prompt, x_sample, c_prompt, c_sample, state_conv, g_norm, w_ada, b_ada, w_in, w_s, b_s, g_v, b_v, w_conv, b_conv, g_cn, b_cn, w_out, g_final):
    n_p, seq, d = x_prompt.shape
    n_s = x_sample.shape[0]
    depth = w_in.shape[0]
    assert seq % TM == 0 and TM % CHUNK == 0 and x_sample.shape[1] == 1
    assert d == D_MODEL and w_in.shape[2] == D_IN

    r3 = lambda a: a.reshape(depth, 1, a.shape[-1])
    gn, gv, bv, bc, gcn, bcn = map(r3, (g_norm, g_v, b_v, b_conv, g_cn, b_cn))
    gfin = g_final.reshape(1, d)
    bst = jnp.swapaxes(b_s, 1, 2)
    ws0 = r3(jnp.repeat(w_s[:, :, 0, 0], HD, axis=-1))
    bs0 = r3(jnp.repeat(b_s[:, :, 0], HD, axis=-1))
    wlast = r3(w_conv[:, CONV_W - 1, :])

    c_p = jnp.pad(c_prompt, ((0, (-n_p) % 8), (0, 0)))
    mod_s, mod_p = _ada_call(c_sample, c_p, w_ada, b_ada)
    state_t = jnp.swapaxes(state_conv, 1, 2)
    cpart = _state_dot_call(state_t, w_conv)

    xp = x_prompt.reshape(n_p * seq, d)
    xs = x_sample.reshape(n_s, d)
    vp_l, cp_l, vs_l, gs_l = [], [], [], []
    for l in range(depth):
        final = l == depth - 1
        xp, vp, gp = _prompt_layer(xp, mod_p, gn, w_in, w_out, w_s, bst, gv, bv, w_conv, bc, gcn, bcn, gfin,
                                   layer=l, seq=seq, final=final)
        xs, vs, gs = _sample_layer(xs, mod_s, gn, w_in, w_out, ws0, bs0, gv, bv, cpart, wlast, bc, gcn, bcn, gfin,
                                   layer=l, final=final)
        vp_l.append(vp)
        cp_l.append(gp[:, HIST - (CONV_W - 1):, :])
        vs_l.append(vs)
        gs_l.append(gs)
    conv_sample = jnp.swapaxes(_state_shift_call(state_t, jnp.stack(gs_l)), 1, 2)
    return (xp.reshape(n_p, seq, d), xs.reshape(n_s, 1, d),
            jnp.stack(vp_l).reshape(depth, n_p, CHUNK, N_GROUPS, HD), jnp.stack(cp_l),
            jnp.stack(vs_l).reshape(depth, n_s, 1, N_GROUPS, HD), conv_sample)
```

```python
import functools

import jax
import jax.numpy as jnp
from jax import lax
from jax.experimental import pallas as pl
from jax.experimental.pallas import tpu as pltpu

D_MODEL = 2048
W_A = 1024
W_B = 1024
HD = 128
N_GROUPS = W_A // HD
CHUNK = 128
CONV_W = 31
D_IN = 3 * W_A + 3 * W_B
EPS = 1e-6
INV_SQRT2 = 0.7071067811865476

BLK_U, BLK_V, BLK_ZA, BLK_A, BLK_B, BLK_ZB = range(6)
N_IN_BLK = D_IN // W_A
N_OUT_BLK = D_MODEL // W_A

TM = 256
HIST = 32
CONV_ROWS = 64
N_STAGE = 4
SAMPLE_BN = 512
VMEM_LIMIT_BYTES = 60 * 1024 * 1024

F32 = jnp.float32
BF16 = jnp.bfloat16


def _gelu(x):
    return 0.5 * x * (1.0 + lax.erf(x * INV_SQRT2))


def _silu(x):
    return x * jax.nn.sigmoid(x)


def _ln_lanes(x, g, b):
    mu = jnp.mean(x, axis=-1, keepdims=True)
    xc = x - mu
    var = jnp.mean(xc * xc, axis=-1, keepdims=True)
    return xc * lax.rsqrt(var + EPS) * g + b


def _rms(x, g):
    return x * lax.rsqrt(jnp.mean(x * x, axis=-1, keepdims=True) + EPS) * g


def _dot(a, b):
    return jnp.dot(a, b, preferred_element_type=F32)


def _ada_kernel(cs_ref, cp_ref, w_ref, b_ref, os_ref, op_ref):
    w = w_ref[...].astype(BF16)
    os_ref[...] = _dot(_silu(cs_ref[...]).astype(BF16), w) + b_ref[...]
    op_ref[...] = _dot(_silu(cp_ref[...]).astype(BF16), w) + b_ref[...]


def _ada_call(c_s, c_p, w_ada, b_ada):
    depth, d, n3 = w_ada.shape
    ns, npad = c_s.shape[0], c_p.shape[0]
    bn = 512
    return pl.pallas_call(
        _ada_kernel,
        grid=(depth, n3 // bn),
        in_specs=[
            pl.BlockSpec((ns, d), lambda l, j: (0, 0)),
            pl.BlockSpec((npad, d), lambda l, j: (0, 0)),
            pl.BlockSpec((None, d, bn), lambda l, j: (l, 0, j)),
            pl.BlockSpec((None, 1, bn), lambda l, j: (l, 0, j)),
        ],
        out_specs=[
            pl.BlockSpec((None, ns, bn), lambda l, j: (l, 0, j)),
            pl.BlockSpec((None, npad, bn), lambda l, j: (l, 0, j)),
        ],
        out_shape=[jax.ShapeDtypeStruct((depth, ns, n3), F32), jax.ShapeDtypeStruct((depth, npad, n3), F32)],
        compiler_params=pltpu.CompilerParams(dimension_semantics=("arbitrary", "arbitrary")),
        name="ada_mod",
    )(c_s, c_p, w_ada, b_ada.reshape(depth, 1, n3))


def _load_weight_blocks(w_hbm, layer, w_scr, blk0, n_blk, stage, sem):
    rows_per_blk = w_scr.shape[1] // TM
    n = n_blk * rows_per_blk

    def copy(i, slot):
        c = i // rows_per_blk
        r = i % rows_per_blk
        src = w_hbm.at[layer, pl.ds(pl.multiple_of(r * TM, TM), TM), pl.ds(pl.multiple_of(c * W_A, W_A), W_A)]
        return pltpu.make_async_copy(src, stage.at[slot], sem.at[slot])

    for i in range(N_STAGE - 1):
        copy(jnp.int32(i), i).start()

    def body(i, carry):
        slot = i % N_STAGE
        nxt = i + N_STAGE - 1

        @pl.when(nxt < n)
        def _():
            copy(nxt, nxt % N_STAGE).start()

        copy(i, slot).wait()
        c = i // rows_per_blk
        r = i % rows_per_blk
        w_scr[blk0 + c, pl.ds(pl.multiple_of(r * TM, TM), TM), :] = stage[slot].astype(BF16)
        return carry

    lax.fori_loop(0, n, body, 0)


def _prompt_kernel(x_ref, mod_ref, gn_ref, win_hbm, wout_hbm, ws_ref, bst_ref, gv_ref, bv_ref,
                   wc_ref, bc_ref, gcn_ref, bcn_ref, gfin_ref,
                   y_ref, vn_out_ref, glu_out_ref,
                   w_scr, act_scr, h_scr, glu_scr, mix_scr, wsm_scr, zero_scr, sem,
                   *, layer, tiles_per_seq, final):
    t = pl.program_id(0)
    tm = x_ref.shape[0]
    n_chunks = tm // CHUNK
    ug_scr, vn_scr, za_scr, zb_scr = (act_scr.at[i] for i in range(4))

    @pl.when(t == 0)
    def _():
        _load_weight_blocks(win_hbm, layer, w_scr, 0, N_IN_BLK, act_scr, sem)
        _load_weight_blocks(wout_hbm, layer, w_scr, N_IN_BLK, N_OUT_BLK, act_scr, sem)
        row = lax.broadcasted_iota(jnp.int32, (CHUNK, CHUNK), 0)
        col = lax.broadcasted_iota(jnp.int32, (CHUNK, CHUNK), 1)
        for h in range(N_GROUPS):
            wsm_scr[h] = jnp.where(row >= col, ws_ref[h], 0.0).astype(BF16)
        zero_scr[0] = 0

    @pl.when(t % tiles_per_seq == 0)
    def _():
        glu_scr[:, 0:HIST, :] = jnp.zeros((N_GROUPS, HIST, HD), F32)

    mrow = mod_ref[pl.ds(t // tiles_per_seq, 1), :]
    shift = mrow[:, 0:D_MODEL]
    scale = mrow[:, D_MODEL:2 * D_MODEL]
    gate = mrow[:, 2 * D_MODEL:3 * D_MODEL]

    h_scr[...] = (_rms(x_ref[...], gn_ref[...]) * (1.0 + scale) + shift).astype(BF16)
    hb = h_scr[...]

    def proj(blk):
        return _dot(hb, w_scr[blk])

    glu = proj(BLK_A) * jax.nn.sigmoid(proj(BLK_B))
    for g in range(N_GROUPS):
        glu_scr[g, HIST:HIST + tm, :] = glu[:, g * HD:(g + 1) * HD]
    zb_scr[...] = _silu(proj(BLK_ZB))
    ug_scr[...] = _gelu(proj(BLK_U))
    vg = _gelu(proj(BLK_V))
    for h in range(N_GROUPS):
        hs = slice(h * HD, (h + 1) * HD)
        vn_scr[:, hs] = _ln_lanes(vg[:, hs], gv_ref[:, hs], bv_ref[:, hs])
    za_scr[...] = _silu(proj(BLK_ZA))

    base = zero_scr[0] + (HIST - (CONV_W - 1))
    for g in range(N_GROUPS):
        gs = slice(g * HD, (g + 1) * HD)
        taps = [wc_ref[k:k + 1, gs] for k in range(CONV_W)]
        for rb in range(tm // CONV_ROWS):
            r0 = rb * CONV_ROWS
            acc = glu_scr[g, pl.ds(base + r0, CONV_ROWS), :] * taps[0]
            for k in range(1, CONV_W):
                acc = acc + glu_scr[g, pl.ds(base + (r0 + k), CONV_ROWS), :] * taps[k]
            yb = _silu(_ln_lanes(acc + bc_ref[:, gs], gcn_ref[:, gs], bcn_ref[:, gs]))
            mix_scr[r0:r0 + CONV_ROWS, W_A + g * HD:W_A + (g + 1) * HD] = (
                yb * zb_scr[r0:r0 + CONV_ROWS, gs]).astype(BF16)

    for h in range(N_GROUPS):
        hs = slice(h * HD, (h + 1) * HD)
        v_all = jnp.concatenate(
            [vn_scr[c * CHUNK:(c + 1) * CHUNK, hs] for c in range(n_chunks)], axis=1).astype(BF16)
        s_all = _dot(wsm_scr[h], v_all)
        bias = bst_ref[:, h:h + 1]
        for c in range(n_chunks):
            rs = slice(c * CHUNK, (c + 1) * CHUNK)
            sp = s_all[:, c * CHUNK:(c + 1) * CHUNK] + bias
            mix_scr[rs, hs] = (ug_scr[rs, hs] * sp * za_scr[rs, hs]).astype(BF16)

    mix = mix_scr[...]
    for n in range(N_OUT_BLK):
        ns = slice(n * W_A, (n + 1) * W_A)
        y_ref[:, ns] = x_ref[:, ns] + gate[:, ns] * _dot(mix, w_scr[N_IN_BLK + n])
    if final:
        y_ref[...] = _rms(y_ref[...], gfin_ref[...])

    vn_out_ref[...] = vn_scr[tm - CHUNK:tm, :]
    for g in range(N_GROUPS):
        tail = glu_scr[g, tm:tm + HIST, :]
        glu_out_ref[:, g * HD:(g + 1) * HD] = tail
        glu_scr[g, 0:HIST, :] = tail


def _prompt_layer(x2d, mod_p, gn, w_in, w_out, ws, bst, gv, bv, wc, bc, gcn, bcn, gfin, *, layer, seq, final):
    n_tok, d = x2d.shape
    n_seq = n_tok // seq
    tiles_per_seq = seq // TM
    row = lambda a: pl.BlockSpec((None, 1, a.shape[-1]), lambda t: (layer, 0, 0))
    in_specs = [
        pl.BlockSpec((TM, d), lambda t: (t, 0)),
        pl.BlockSpec((None,) + mod_p.shape[1:], lambda t: (layer, 0, 0)),
        row(gn),
        pl.BlockSpec(memory_space=pl.ANY),
        pl.BlockSpec(memory_space=pl.ANY),
        pl.BlockSpec((None, N_GROUPS, CHUNK, CHUNK), lambda t: (layer, 0, 0, 0)),
        pl.BlockSpec((None, CHUNK, N_GROUPS), lambda t: (layer, 0, 0)),
        row(gv), row(bv),
        pl.BlockSpec((None, CONV_W, W_B), lambda t: (layer, 0, 0)),
        row(bc), row(gcn), row(bcn),
        pl.BlockSpec((1, d), lambda t: (0, 0)),
    ]
    out_specs = [
        pl.BlockSpec((TM, d), lambda t: (t, 0)),
        pl.BlockSpec((None, CHUNK, W_A), lambda t: (t // tiles_per_seq, 0, 0)),
        pl.BlockSpec((None, HIST, W_B), lambda t: (t // tiles_per_seq, 0, 0)),
    ]
    out_shape = [
        jax.ShapeDtypeStruct((n_tok, d), F32),
        jax.ShapeDtypeStruct((n_seq, CHUNK, W_A), F32),
        jax.ShapeDtypeStruct((n_seq, HIST, W_B), F32),
    ]
    scratch = [
        pltpu.VMEM((N_IN_BLK + N_OUT_BLK, d, W_A), BF16),
        pltpu.VMEM((N_STAGE, TM, W_A), F32),
        pltpu.VMEM((TM, d), BF16),
        pltpu.VMEM((N_GROUPS, HIST + TM, HD), F32),
        pltpu.VMEM((TM, W_A + W_B), BF16),
        pltpu.VMEM((N_GROUPS, CHUNK, CHUNK), BF16),
        pltpu.SMEM((1,), jnp.int32),
        pltpu.SemaphoreType.DMA((N_STAGE,)),
    ]
    return pl.pallas_call(
        functools.partial(_prompt_kernel, layer=layer, tiles_per_seq=tiles_per_seq, final=final),
        grid=(n_tok // TM,),
        in_specs=in_specs,
        out_specs=out_specs,
        out_shape=out_shape,
        scratch_shapes=scratch,
        compiler_params=pltpu.CompilerParams(
            dimension_semantics=("arbitrary",), vmem_limit_bytes=VMEM_LIMIT_BYTES),
        name=f"prompt_layer{layer}",
    )(x2d, mod_p, gn, w_in, w_out, ws, bst, gv, bv, wc, bc, gcn, bcn, gfin)


STATE_BN = 256


def _state_dot_kernel(s_ref, wc_ref, o_ref):
    acc = s_ref[0] * wc_ref[0:1, :]
    for k in range(1, CONV_W - 1):
        acc = acc + s_ref[k] * wc_ref[k:k + 1, :]
    o_ref[...] = acc


def _state_shift_kernel(s_ref, g_ref, o_ref):
    n_rows = s_ref.shape[0]
    o_ref[0:n_rows - 1] = s_ref[1:n_rows]
    o_ref[n_rows - 1] = g_ref[...]


def _state_dot_call(state_t, w_conv):
    depth, rows, n, w = state_t.shape
    return pl.pallas_call(
        _state_dot_kernel,
        grid=(depth, w // STATE_BN),
        in_specs=[
            pl.BlockSpec((None, rows, n, STATE_BN), lambda l, j: (l, 0, 0, j)),
            pl.BlockSpec((None, CONV_W, STATE_BN), lambda l, j: (l, 0, j)),
        ],
        out_specs=pl.BlockSpec((None, n, STATE_BN), lambda l, j: (l, 0, j)),
        out_shape=jax.ShapeDtypeStruct((depth, n, w), F32),
        compiler_params=pltpu.CompilerParams(dimension_semantics=("arbitrary", "arbitrary")),
        name="sample_conv_state",
    )(state_t, w_conv)


def _state_shift_call(state_t, glu):
    depth, rows, n, w = state_t.shape
    return pl.pallas_call(
        _state_shift_kernel,
        grid=(depth, w // STATE_BN),
        in_specs=[
            pl.BlockSpec((None, rows, n, STATE_BN), lambda l, j: (l, 0, 0, j)),
            pl.BlockSpec((None, n, STATE_BN), lambda l, j: (l, 0, j)),
        ],
        out_specs=pl.BlockSpec((None, rows, n, STATE_BN), lambda l, j: (l, 0, 0, j)),
        out_shape=jax.ShapeDtypeStruct(state_t.shape, F32),
        compiler_params=pltpu.CompilerParams(dimension_semantics=("arbitrary", "arbitrary")),
        name="sample_state_shift",
    )(state_t, glu)


def _sample_kernel(x_ref, mod_ref, gn_ref, win_ref, wout_ref, ws0_ref, bs0_ref, gv_ref, bv_ref,
                   cpart_ref, wlast_ref, bc_ref, gcn_ref, bcn_ref, gfin_ref,
                   y_ref, vn_out_ref, glu_out_ref,
                   h_scr, proj_scr, mix_scr, y_scr, *, n_in_steps, n_out_steps, final):
    j = pl.program_id(0)
    per_blk = W_A // SAMPLE_BN

    @pl.when(j == 0)
    def _():
        shift = mod_ref[:, 0:D_MODEL]
        scale = mod_ref[:, D_MODEL:2 * D_MODEL]
        h_scr[...] = (_rms(x_ref[...], gn_ref[...]) * (1.0 + scale) + shift).astype(BF16)

    @pl.when(j < n_in_steps)
    def _():
        proj_scr[j] = _dot(h_scr[...], win_ref[...].astype(BF16))

    @pl.when(j == n_in_steps)
    def _():
        def blk(b):
            return jnp.concatenate([proj_scr[b * per_blk + i] for i in range(per_blk)], axis=1)

        glu = blk(BLK_A) * jax.nn.sigmoid(blk(BLK_B))
        glu_out_ref[...] = glu
        zb = _silu(blk(BLK_ZB))
        ug = _gelu(blk(BLK_U))
        vg = _gelu(blk(BLK_V))
        za = _silu(blk(BLK_ZA))
        conv = cpart_ref[...] + glu * wlast_ref[...] + bc_ref[...]
        for h in range(N_GROUPS):
            hs = slice(h * HD, (h + 1) * HD)
            vn = _ln_lanes(vg[:, hs], gv_ref[:, hs], bv_ref[:, hs])
            vn_out_ref[:, hs] = vn
            sp = vn * ws0_ref[:, hs] + bs0_ref[:, hs]
            mix_scr[:, hs] = (ug[:, hs] * sp * za[:, hs]).astype(BF16)
            yb = _silu(_ln_lanes(conv[:, hs], gcn_ref[:, hs], bcn_ref[:, hs]))
            mix_scr[:, W_A + h * HD:W_A + (h + 1) * HD] = (yb * zb[:, hs]).astype(BF16)

    @pl.when(j >= n_in_steps)
    def _():
        y_scr[j - n_in_steps] = _dot(mix_scr[...], wout_ref[...].astype(BF16))

    @pl.when(j == n_in_steps + n_out_steps - 1)
    def _():
        gate = mod_ref[:, 2 * D_MODEL:3 * D_MODEL]
        y = x_ref[...] + gate * jnp.concatenate([y_scr[i] for i in range(n_out_steps)], axis=1)
        if final:
            y = _rms(y, gfin_ref[...])
        y_ref[...] = y


def _sample_layer(x2d, mod_s, gn, w_in, w_out, ws0, bs0, gv, bv, cpart, wlast, bc, gcn, bcn, gfin, *, layer, final):
    n, d = x2d.shape
    n_in_steps = D_IN // SAMPLE_BN
    n_out_steps = d // SAMPLE_BN
    row = lambda a: pl.BlockSpec((None, 1, a.shape[-1]), lambda j: (layer, 0, 0))
    full = lambda a: pl.BlockSpec(a.shape, lambda j: (0,) * a.ndim)
    in_specs = [
        full(x2d),
        pl.BlockSpec((None, n, 3 * d), lambda j: (layer, 0, 0)),
        row(gn),
        pl.BlockSpec((None, d, SAMPLE_BN), lambda j: (layer, 0, jnp.minimum(j, n_in_steps - 1))),
        pl.BlockSpec((None, d, SAMPLE_BN), lambda j: (layer, 0, jnp.maximum(j - n_in_steps, 0))),
        row(ws0), row(bs0), row(gv), row(bv),
        pl.BlockSpec((None, n, W_B), lambda j: (layer, 0, 0)),
        row(wlast), row(bc), row(gcn), row(bcn),
        full(gfin),
    ]
    out_shape = [
        jax.ShapeDtypeStruct((n, d), F32),
        jax.ShapeDtypeStruct((n, W_A), F32),
        jax.ShapeDtypeStruct((n, W_B), F32),
    ]
    scratch = [
        pltpu.VMEM((n, d), BF16),
        pltpu.VMEM((n_in_steps, n, SAMPLE_BN), F32),
        pltpu.VMEM((n, W_A + W_B), BF16),
        pltpu.VMEM((n_out_steps, n, SAMPLE_BN), F32),
    ]
    return pl.pallas_call(
        functools.partial(_sample_kernel, n_in_steps=n_in_steps, n_out_steps=n_out_steps, final=final),
        grid=(n_in_steps + n_out_steps,),
        in_specs=in_specs,
        out_specs=[full(s) for s in out_shape],
        out_shape=out_shape,
        scratch_shapes=scratch,
        compiler_params=pltpu.CompilerParams(dimension_semantics=("arbitrary",)),
        name=f"sample_layer{layer}",
    )(x2d, mod_s, gn, w_in, w_out, ws0, bs0, gv, bv, cpart, wlast, bc, gcn, bcn, gfin)


def kernel(x_prompt, x_sample, c_prompt, c_sample, state_conv, g_norm, w_ada, b_ada, w_in, w_s, b_s, g_v, b_v, w_conv, b_conv, g_cn, b_cn, w_out, g_final):
    n_p, seq, d = x_prompt.shape
    n_s = x_sample.shape[0]
    depth = w_in.shape[0]
    assert seq % TM == 0 and TM % CHUNK == 0 and x_sample.shape[1] == 1
    assert d == D_MODEL and w_in.shape[2] == D_IN

    r3 = lambda a: a.reshape(depth, 1, a.shape[-1])
    gn, gv, bv, bc, gcn, bcn = map(r3, (g_norm, g_v, b_v, b_conv, g_cn, b_cn))
    gfin = g_final.reshape(1, d)
    bst = jnp.swapaxes(b_s, 1, 2)
    ws0 = r3(jnp.repeat(w_s[:, :, 0, 0], HD, axis=-1))
    bs0 = r3(jnp.repeat(b_s[:, :, 0], HD, axis=-1))
    wlast = r3(w_conv[:, CONV_W - 1, :])

    c_p = jnp.pad(c_prompt, ((0, (-n_p) % 8), (0, 0)))
    mod_s, mod_p = _ada_call(c_sample, c_p, w_ada, b_ada)
    state_t = jnp.swapaxes(state_conv, 1, 2)
    cpart = _state_dot_call(state_t, w_conv)

    xp = x_prompt.reshape(n_p * seq, d)
    xs = x_sample.reshape(n_s, d)
    vp_l, cp_l, vs_l, gs_l = [], [], [], []
    for l in range(depth):
        final = l == depth - 1
        xp, vp, gp = _prompt_layer(xp, mod_p, gn, w_in, w_out, w_s, bst, gv, bv, w_conv, bc, gcn, bcn, gfin,
                                   layer=l, seq=seq, final=final)
        xs, vs, gs = _sample_layer(xs, mod_s, gn, w_in, w_out, ws0, bs0, gv, bv, cpart, wlast, bc, gcn, bcn, gfin,
                                   layer=l, final=final)
        vp_l.append(vp)
        cp_l.append(gp[:, HIST - (CONV_W - 1):, :])
        vs_l.append(vs)
        gs_l.append(gs)
    conv_sample = jnp.swapaxes(_state_shift_call(state_t, jnp.stack(gs_l)), 1, 2)
    return (xp.reshape(n_p, seq, d), xs.reshape(n_s, 1, d),
            jnp.stack(vp_l).reshape(depth, n_p, CHUNK, N_GROUPS, HD), jnp.stack(cp_l),
            jnp.stack(vs_l).reshape(depth, n_s, 1, N_GROUPS, HD), conv_sample)
```

```python
import functools

import jax
import jax.numpy as jnp
from jax import lax
from jax.experimental import pallas as pl
from jax.experimental.pallas import tpu as pltpu

D_MODEL = 2048
W_A = 1024
W_B = 1024
HD = 128
N_GROUPS = W_A // HD
CHUNK = 128
CONV_W = 31
D_IN = 3 * W_A + 3 * W_B
EPS = 1e-6
INV_SQRT2 = 0.7071067811865476

BLK_U, BLK_V, BLK_ZA, BLK_A, BLK_B, BLK_ZB = range(6)
N_IN_BLK = D_IN // W_A
N_OUT_BLK = D_MODEL // W_A

TM = 256
HIST = 32
CONV_ROWS = 64
N_STAGE = 4
N_SAMPLE_IN = 2 + 6 + 1
N_SAMPLE_OUT = 4
ADA_BK = 256
STATE_BN = 256
VMEM_LIMIT_BYTES = 60 * 1024 * 1024

F32 = jnp.float32
BF16 = jnp.bfloat16


def _gelu(x):
    return 0.5 * x * (1.0 + lax.erf(x * INV_SQRT2))


def _silu(x):
    return x * jax.nn.sigmoid(x)


def _ln_lanes(x, g, b):
    mu = jnp.mean(x, axis=-1, keepdims=True)
    xc = x - mu
    var = jnp.mean(xc * xc, axis=-1, keepdims=True)
    return xc * lax.rsqrt(var + EPS) * g + b


def _rms(x, g):
    return x * lax.rsqrt(jnp.mean(x * x, axis=-1, keepdims=True) + EPS) * g


def _dot(a, b):
    return jnp.dot(a, b, preferred_element_type=F32)


def _ada_kernel(cs_ref, cp_ref, w_ref, b_ref, os_ref, op_ref):
    k = pl.program_id(1)
    ns = cs_ref.shape[0]
    c = jnp.concatenate([_silu(cs_ref[...]), _silu(cp_ref[...])], axis=0).astype(BF16)
    part = _dot(c, w_ref[...].astype(BF16))

    @pl.when(k == 0)
    def _():
        os_ref[...] = part[0:ns] + b_ref[...]
        op_ref[...] = part[ns:] + b_ref[...]

    @pl.when(k > 0)
    def _():
        os_ref[...] += part[0:ns]
        op_ref[...] += part[ns:]


def _ada_call(c_s, c_p, w_ada, b_ada):
    depth, d, n3 = w_ada.shape
    ns, npad = c_s.shape[0], c_p.shape[0]
    return pl.pallas_call(
        _ada_kernel,
        grid=(depth, d // ADA_BK),
        in_specs=[
            pl.BlockSpec((ns, ADA_BK), lambda l, k: (0, k)),
            pl.BlockSpec((npad, ADA_BK), lambda l, k: (0, k)),
            pl.BlockSpec((None, ADA_BK, n3), lambda l, k: (l, k, 0)),
            pl.BlockSpec((None, 1, n3), lambda l, k: (l, 0, 0)),
        ],
        out_specs=[
            pl.BlockSpec((None, ns, n3), lambda l, k: (l, 0, 0)),
            pl.BlockSpec((None, npad, n3), lambda l, k: (l, 0, 0)),
        ],
        out_shape=[jax.ShapeDtypeStruct((depth, ns, n3), F32), jax.ShapeDtypeStruct((depth, npad, n3), F32)],
        compiler_params=pltpu.CompilerParams(dimension_semantics=("arbitrary", "arbitrary")),
        name="ada_mod",
    )(c_s, c_p, w_ada, b_ada.reshape(depth, 1, n3))


def _load_weight_blocks(w_hbm, layer, w_scr, blk0, n_blk, stage, sem):
    rows_per_blk = w_scr.shape[1] // TM
    n = n_blk * rows_per_blk

    def copy(i, slot):
        c = i // rows_per_blk
        r = i % rows_per_blk
        src = w_hbm.at[layer, pl.ds(pl.multiple_of(r * TM, TM), TM), pl.ds(pl.multiple_of(c * W_A, W_A), W_A)]
        return pltpu.make_async_copy(src, stage.at[slot], sem.at[slot])

    for i in range(N_STAGE - 1):
        copy(jnp.int32(i), i).start()

    def body(i, carry):
        slot = i % N_STAGE
        nxt = i + N_STAGE - 1

        @pl.when(nxt < n)
        def _():
            copy(nxt, nxt % N_STAGE).start()

        copy(i, slot).wait()
        c = i // rows_per_blk
        r = i % rows_per_blk
        w_scr[blk0 + c, pl.ds(pl.multiple_of(r * TM, TM), TM), :] = stage[slot].astype(BF16)
        return carry

    lax.fori_loop(0, n, body, 0)


def _prompt_tile(t, x_ref, mod_ref, gn_ref, ws_ref, bst_ref, gv_ref, bv_ref, wc_ref, bc_ref, gcn_ref, bcn_ref,
                 gfin_ref, y_ref, vn_out_ref, glu_out_ref, w_scr, act_scr, h_scr, glu_scr, mix_scr, wsm_scr,
                 zero_scr, *, tiles_per_seq, final):
    tm = x_ref.shape[0]
    n_chunks = tm // CHUNK
    ug_scr, vn_scr, za_scr, zb_scr = (act_scr.at[i] for i in range(4))

    mrow = mod_ref[pl.ds(t // tiles_per_seq, 1), :]
    shift = mrow[:, 0:D_MODEL]
    scale = mrow[:, D_MODEL:2 * D_MODEL]
    gate = mrow[:, 2 * D_MODEL:3 * D_MODEL]

    h_scr[...] = (_rms(x_ref[...], gn_ref[...]) * (1.0 + scale) + shift).astype(BF16)
    hb = h_scr[...]

    def proj(blk):
        return _dot(hb, w_scr[blk])

    glu = proj(BLK_A) * jax.nn.sigmoid(proj(BLK_B))
    for g in range(N_GROUPS):
        glu_scr[g, HIST:HIST + tm, :] = glu[:, g * HD:(g + 1) * HD]
    zb_scr[...] = _silu(proj(BLK_ZB))
    ug_scr[...] = _gelu(proj(BLK_U))
    vg = _gelu(proj(BLK_V))
    for h in range(N_GROUPS):
        hs = slice(h * HD, (h + 1) * HD)
        vn_scr[:, hs] = _ln_lanes(vg[:, hs], gv_ref[:, hs], bv_ref[:, hs])
    za_scr[...] = _silu(proj(BLK_ZA))

    base = zero_scr[0] + (HIST - (CONV_W - 1))
    for g in range(N_GROUPS):
        gs = slice(g * HD, (g + 1) * HD)
        taps = [wc_ref[k:k + 1, gs] for k in range(CONV_W)]
        for rb in range(tm // CONV_ROWS):
            r0 = rb * CONV_ROWS
            acc = glu_scr[g, pl.ds(base + r0, CONV_ROWS), :] * taps[0]
            for k in range(1, CONV_W):
                acc = acc + glu_scr[g, pl.ds(base + (r0 + k), CONV_ROWS), :] * taps[k]
            yb = _silu(_ln_lanes(acc + bc_ref[:, gs], gcn_ref[:, gs], bcn_ref[:, gs]))
            mix_scr[r0:r0 + CONV_ROWS, W_A + g * HD:W_A + (g + 1) * HD] = (
                yb * zb_scr[r0:r0 + CONV_ROWS, gs]).astype(BF16)

    for h in range(N_GROUPS):
        hs = slice(h * HD, (h + 1) * HD)
        v_all = jnp.concatenate(
            [vn_scr[c * CHUNK:(c + 1) * CHUNK, hs] for c in range(n_chunks)], axis=1).astype(BF16)
        s_all = _dot(wsm_scr[h], v_all)
        bias = bst_ref[:, h:h + 1]
        for c in range(n_chunks):
            rs = slice(c * CHUNK, (c + 1) * CHUNK)
            sp = s_all[:, c * CHUNK:(c + 1) * CHUNK] + bias
            mix_scr[rs, hs] = (ug_scr[rs, hs] * sp * za_scr[rs, hs]).astype(BF16)

    mix = mix_scr[...]
    for n in range(N_OUT_BLK):
        ns = slice(n * W_A, (n + 1) * W_A)
        y_ref[:, ns] = x_ref[:, ns] + gate[:, ns] * _dot(mix, w_scr[N_IN_BLK + n])
    if final:
        y_ref[...] = _rms(y_ref[...], gfin_ref[...])

    vn_out_ref[...] = vn_scr[tm - CHUNK:tm, :]
    for g in range(N_GROUPS):
        tail = glu_scr[g, tm:tm + HIST, :]
        glu_out_ref[:, g * HD:(g + 1) * HD] = tail
        glu_scr[g, 0:HIST, :] = tail


def _sample_step(layer, gn_ref, gv_ref, bv_ref, wc_ref, bc_ref, gcn_ref, bcn_ref, gfin_ref, ws0_ref, bs0_ref,
                 xs_hbm, mods_hbm, cpart_hbm, ys_hbm, vns_hbm, glus_hbm,
                 w_scr, act_scr, h_scr, mix_scr, cpart_scr, ssem, *, final):
    n = xs_hbm.shape[0]

    def half(slot, j):
        return act_scr.at[slot, pl.ds(j * n, n), :]

    def cols(ref, j):
        return ref.at[:, pl.ds(j * W_A, W_A)]

    copies = [pltpu.make_async_copy(cols(xs_hbm, j), half(0, j), ssem.at[j]) for j in range(2)]
    copies += [pltpu.make_async_copy(cols(mods_hbm.at[layer], j), half(1 + j // 2, j % 2), ssem.at[2 + j])
               for j in range(6)]
    copies.append(pltpu.make_async_copy(cpart_hbm.at[layer], cpart_scr, ssem.at[8]))
    assert len(copies) == N_SAMPLE_IN
    for c in copies:
        c.start()
    for c in copies:
        c.wait()

    def wide(slot):
        return jnp.concatenate([act_scr[slot, 0:n, :], act_scr[slot, n:2 * n, :]], axis=1)

    x, shift, scale, gate = wide(0), wide(1), wide(2), wide(3)
    h_scr[0:n, :] = (_rms(x, gn_ref[...]) * (1.0 + scale) + shift).astype(BF16)
    hb = h_scr[0:n, :]

    def proj(blk):
        return _dot(hb, w_scr[blk])

    glu = proj(BLK_A) * jax.nn.sigmoid(proj(BLK_B))
    zb = _silu(proj(BLK_ZB))
    ug = _gelu(proj(BLK_U))
    vg = _gelu(proj(BLK_V))
    za = _silu(proj(BLK_ZA))
    conv = cpart_scr[...] + glu * wc_ref[CONV_W - 1:CONV_W, :] + bc_ref[...]
    vns = []
    for h in range(N_GROUPS):
        hs = slice(h * HD, (h + 1) * HD)
        vn = _ln_lanes(vg[:, hs], gv_ref[:, hs], bv_ref[:, hs])
        vns.append(vn)
        sp = vn * ws0_ref[:, hs] + bs0_ref[:, hs]
        mix_scr[0:n, hs] = (ug[:, hs] * sp * za[:, hs]).astype(BF16)
        yb = _silu(_ln_lanes(conv[:, hs], gcn_ref[:, hs], bcn_ref[:, hs]))
        mix_scr[0:n, W_A + h * HD:W_A + (h + 1) * HD] = (yb * zb[:, hs]).astype(BF16)
    mix = mix_scr[0:n, :]
    y = x + gate * jnp.concatenate([_dot(mix, w_scr[N_IN_BLK + i]) for i in range(N_OUT_BLK)], axis=1)
    if final:
        y = _rms(y, gfin_ref[...])

    for j in range(2):
        act_scr[0, j * n:(j + 1) * n, :] = y[:, j * W_A:(j + 1) * W_A]
    act_scr[1, 0:n, :] = glu
    act_scr[1, n:2 * n, :] = jnp.concatenate(vns, axis=1)
    outs = [pltpu.make_async_copy(half(0, j), cols(ys_hbm, j), ssem.at[N_SAMPLE_IN + j]) for j in range(2)]
    outs.append(pltpu.make_async_copy(half(1, 0), glus_hbm, ssem.at[N_SAMPLE_IN + 2]))
    outs.append(pltpu.make_async_copy(half(1, 1), vns_hbm, ssem.at[N_SAMPLE_IN + 3]))
    assert len(outs) == N_SAMPLE_OUT
    for c in outs:
        c.start()
    for c in outs:
        c.wait()


def _layer_kernel(x_ref, mod_ref, gn_ref, win_hbm, wout_hbm, ws_ref, bst_ref, gv_ref, bv_ref,
                  wc_ref, bc_ref, gcn_ref, bcn_ref, gfin_ref, ws0_ref, bs0_ref, xs_hbm, mods_hbm, cpart_hbm,
                  y_ref, vn_out_ref, glu_out_ref, ys_hbm, vns_hbm, glus_hbm,
                  w_scr, act_scr, h_scr, glu_scr, mix_scr, wsm_scr, cpart_scr, zero_scr, sem, ssem,
                  *, layer, n_tiles, tiles_per_seq, final):
    t = pl.program_id(0)

    @pl.when(t == 0)
    def _():
        _load_weight_blocks(win_hbm, layer, w_scr, 0, N_IN_BLK, act_scr, sem)
        _load_weight_blocks(wout_hbm, layer, w_scr, N_IN_BLK, N_OUT_BLK, act_scr, sem)
        row = lax.broadcasted_iota(jnp.int32, (CHUNK, CHUNK), 0)
        col = lax.broadcasted_iota(jnp.int32, (CHUNK, CHUNK), 1)
        for h in range(N_GROUPS):
            wsm_scr[h] = jnp.where(row >= col, ws_ref[h], 0.0).astype(BF16)
        zero_scr[0] = 0

    @pl.when(jnp.logical_and(t < n_tiles, t % tiles_per_seq == 0))
    def _():
        glu_scr[:, 0:HIST, :] = jnp.zeros((N_GROUPS, HIST, HD), F32)

    @pl.when(t < n_tiles)
    def _():
        _prompt_tile(t, x_ref, mod_ref, gn_ref, ws_ref, bst_ref, gv_ref, bv_ref, wc_ref, bc_ref, gcn_ref,
                     bcn_ref, gfin_ref, y_ref, vn_out_ref, glu_out_ref, w_scr, act_scr, h_scr, glu_scr,
                     mix_scr, wsm_scr, zero_scr, tiles_per_seq=tiles_per_seq, final=final)

    @pl.when(t == n_tiles)
    def _():
        _sample_step(layer, gn_ref, gv_ref, bv_ref, wc_ref, bc_ref, gcn_ref, bcn_ref, gfin_ref, ws0_ref,
                     bs0_ref, xs_hbm, mods_hbm, cpart_hbm, ys_hbm, vns_hbm, glus_hbm,
                     w_scr, act_scr, h_scr, mix_scr, cpart_scr, ssem, final=final)


def _layer_call(x2d, xs, mod_p, mod_s, cpart, gn, w_in, w_out, ws, bst, ws0, bs0, gv, bv, wc, bc, gcn, bcn, gfin,
                *, layer, seq, final):
    n_tok, d = x2d.shape
    n_s = xs.shape[0]
    n_seq = n_tok // seq
    tiles_per_seq = seq // TM
    n_tiles = n_tok // TM
    assert 2 * n_s == TM
    tile = lambda t: jnp.minimum(t, n_tiles - 1)
    row = lambda a: pl.BlockSpec((None, 1, a.shape[-1]), lambda t: (layer, 0, 0))
    hbm = pl.BlockSpec(memory_space=pl.ANY)
    in_specs = [
        pl.BlockSpec((TM, d), lambda t: (tile(t), 0)),
        pl.BlockSpec((None,) + mod_p.shape[1:], lambda t: (layer, 0, 0)),
        row(gn),
        hbm, hbm,
        pl.BlockSpec((None, N_GROUPS, CHUNK, CHUNK), lambda t: (layer, 0, 0, 0)),
        pl.BlockSpec((None, CHUNK, N_GROUPS), lambda t: (layer, 0, 0)),
        row(gv), row(bv),
        pl.BlockSpec((None, CONV_W, W_B), lambda t: (layer, 0, 0)),
        row(bc), row(gcn), row(bcn),
        pl.BlockSpec((1, d), lambda t: (0, 0)),
        row(ws0), row(bs0),
        hbm, hbm, hbm,
    ]
    out_specs = [
        pl.BlockSpec((TM, d), lambda t: (tile(t), 0)),
        pl.BlockSpec((None, CHUNK, W_A), lambda t: (tile(t) // tiles_per_seq, 0, 0)),
        pl.BlockSpec((None, HIST, W_B), lambda t: (tile(t) // tiles_per_seq, 0, 0)),
        hbm, hbm, hbm,
    ]
    out_shape = [
        jax.ShapeDtypeStruct((n_tok, d), F32),
        jax.ShapeDtypeStruct((n_seq, CHUNK, W_A), F32),
        jax.ShapeDtypeStruct((n_seq, HIST, W_B), F32),
        jax.ShapeDtypeStruct((n_s, d), F32),
        jax.ShapeDtypeStruct((n_s, W_A), F32),
        jax.ShapeDtypeStruct((n_s, W_B), F32),
    ]
    scratch = [
        pltpu.VMEM((N_IN_BLK + N_OUT_BLK, d, W_A), BF16),
        pltpu.VMEM((N_STAGE, TM, W_A), F32),
        pltpu.VMEM((TM, d), BF16),
        pltpu.VMEM((N_GROUPS, HIST + TM, HD), F32),
        pltpu.VMEM((TM, W_A + W_B), BF16),
        pltpu.VMEM((N_GROUPS, CHUNK, CHUNK), BF16),
        pltpu.VMEM((n_s, W_B), F32),
        pltpu.SMEM((1,), jnp.int32),
        pltpu.SemaphoreType.DMA((N_STAGE,)),
        pltpu.SemaphoreType.DMA((N_SAMPLE_IN + N_SAMPLE_OUT,)),
    ]
    return pl.pallas_call(
        functools.partial(_layer_kernel, layer=layer, n_tiles=n_tiles, tiles_per_seq=tiles_per_seq, final=final),
        grid=(n_tiles + 1,),
        in_specs=in_specs,
        out_specs=out_specs,
        out_shape=out_shape,
        scratch_shapes=scratch,
        compiler_params=pltpu.CompilerParams(
            dimension_semantics=("arbitrary",), vmem_limit_bytes=VMEM_LIMIT_BYTES),
        name=f"layer{layer}",
    )(x2d, mod_p, gn, w_in, w_out, ws, bst, gv, bv, wc, bc, gcn, bcn, gfin, ws0, bs0, xs, mod_s, cpart)


def _state_dot_kernel(s_ref, wc_ref, o_ref):
    acc = s_ref[0] * wc_ref[0:1, :]
    for k in range(1, CONV_W - 1):
        acc = acc + s_ref[k] * wc_ref[k:k + 1, :]
    o_ref[...] = acc


def _state_shift_kernel(s_ref, g_ref, o_ref):
    n_rows = s_ref.shape[0]
    o_ref[0:n_rows - 1] = s_ref[1:n_rows]
    o_ref[n_rows - 1] = g_ref[...]


def _state_dot_call(state_t, w_conv):
    depth, rows, n, w = state_t.shape
    return pl.pallas_call(
        _state_dot_kernel,
        grid=(depth, w // STATE_BN),
        in_specs=[
            pl.BlockSpec((None, rows, n, STATE_BN), lambda l, j: (l, 0, 0, j)),
            pl.BlockSpec((None, CONV_W, STATE_BN), lambda l, j: (l, 0, j)),
        ],
        out_specs=pl.BlockSpec((None, n, STATE_BN), lambda l, j: (l, 0, j)),
        out_shape=jax.ShapeDtypeStruct((depth, n, w), F32),
        compiler_params=pltpu.CompilerParams(dimension_semantics=("arbitrary", "arbitrary")),
        name="sample_conv_state",
    )(state_t, w_conv)


def _state_shift_call(state_t, glu):
    depth, rows, n, w = state_t.shape
    return pl.pallas_call(
        _state_shift_kernel,
        grid=(depth, w // STATE_BN),
        in_specs=[
            pl.BlockSpec((None, rows, n, STATE_BN), lambda l, j: (l, 0, 0, j)),
            pl.BlockSpec((None, n, STATE_BN), lambda l, j: (l, 0, j)),
        ],
        out_specs=pl.BlockSpec((None, rows, n, STATE_BN), lambda l, j: (l, 0, 0, j)),
        out_shape=jax.ShapeDtypeStruct(state_t.shape, F32),
        compiler_params=pltpu.CompilerParams(dimension_semantics=("arbitrary", "arbitrary")),
        name="sample_state_shift",
    )(state_t, glu)


def kernel(x_prompt, x_sample, c_prompt, c_sample, state_conv, g_norm, w_ada, b_ada, w_in, w_s, b_s, g_v, b_v, w_conv, b_conv, g_cn, b_cn, w_out, g_final):
    n_p, seq, d = x_prompt.shape
    n_s = x_sample.shape[0]
    depth = w_in.shape[0]
    assert seq % TM == 0 and TM % CHUNK == 0 and x_sample.shape[1] == 1
    assert d == D_MODEL and w_in.shape[2] == D_IN

    r3 = lambda a: a.reshape(depth, 1, a.shape[-1])
    gn, gv, bv, bc, gcn, bcn = map(r3, (g_norm, g_v, b_v, b_conv, g_cn, b_cn))
    gfin = g_final.reshape(1, d)
    bst = jnp.swapaxes(b_s, 1, 2)
    ws0 = r3(jnp.repeat(w_s[:, :, 0, 0], HD, axis=-1))
    bs0 = r3(jnp.repeat(b_s[:, :, 0], HD, axis=-1))

    c_p = jnp.pad(c_prompt, ((0, (-n_p) % 8), (0, 0)))
    mod_s, mod_p = _ada_call(c_sample, c_p, w_ada, b_ada)
    state_t = jnp.swapaxes(state_conv, 1, 2)
    cpart = _state_dot_call(state_t, w_conv)

    xp = x_prompt.reshape(n_p * seq, d)
    xs = x_sample.reshape(n_s, d)
    vp_l, cp_l, vs_l, gs_l = [], [], [], []
    for l in range(depth):
        xp, vp, gp, xs, vs, gs = _layer_call(
            xp, xs, mod_p, mod_s, cpart, gn, w_in, w_out, w_s, bst, ws0, bs0, gv, bv, w_conv, bc, gcn, bcn, gfin,
            layer=l, seq=seq, final=l == depth - 1)
        vp_l.append(vp)
        cp_l.append(gp[:, HIST - (CONV_W - 1):, :])
        vs_l.append(vs)
        gs_l.append(gs)
    conv_sample = jnp.swapaxes(_state_shift_call(state_t, jnp.stack(gs_l)), 1, 2)
    return (xp.reshape(n_p, seq, d), xs.reshape(n_s, 1, d),
            jnp.stack(vp_l).reshape(depth, n_p, CHUNK, N_GROUPS, HD), jnp.stack(cp_l),
            jnp.stack(vs_l).reshape(depth, n_s, 1, N_GROUPS, HD), conv_sample)
```
